```python
import jax, jax.numpy as jnp
from jax import lax
import numpy as np

D_MODEL = 1024
BATCH = 16
SEQ = 2048
DEPTH = 4

GRID_W = 64
CTX_LEN = 256
Q_BLOCK = 128
ROPE_BASE = 10000.0
RMS_EPS = 1e-6
LN_EPS = 1e-5

MLA_HEADS = 8
MLA_Q_LORA = 384
MLA_KV_LORA = 256
MLA_NOPE = 64
MLA_ROPE = 32
MLA_V = 64
MLA_QK = MLA_NOPE + MLA_ROPE
MLA_SCALE = MLA_QK ** -0.5

HEAD_DIM = 64
GQA_HEADS = 8
GQA_KV_HEADS = 2
GQA_GROUP = GQA_HEADS // GQA_KV_HEADS
GQA_SCALE = HEAD_DIM ** -0.5

MIX_WIDTH = MLA_HEADS * MLA_V + GQA_HEADS * HEAD_DIM
IN_SPLITS = (MLA_Q_LORA,
             MLA_Q_LORA + MLA_KV_LORA,
             MLA_Q_LORA + MLA_KV_LORA + MLA_ROPE,
             MLA_Q_LORA + MLA_KV_LORA + MLA_ROPE + GQA_HEADS * HEAD_DIM,
             MLA_Q_LORA + MLA_KV_LORA + MLA_ROPE + (GQA_HEADS + GQA_KV_HEADS) * HEAD_DIM)
IN_COLS = MLA_Q_LORA + MLA_KV_LORA + MLA_ROPE + (GQA_HEADS + 2 * GQA_KV_HEADS) * HEAD_DIM

N_GROUPS = 4
EXPERTS_PER_GROUP = 8
N_EXPERTS = N_GROUPS * EXPERTS_PER_GROUP
TOP_K = 2
D_FF_EXPERT = 512
MOE_BLOCK = 128

ALPHA = (2.0 * DEPTH) ** 0.25
BETA = (8.0 * DEPTH) ** -0.25

kernel_name = "hybrid_mla_gqa_hmoe_diffusion_block"


def rms_norm(x, g):
    xf = x.astype(jnp.float32)
    y = xf * lax.rsqrt(jnp.mean(jnp.square(xf), axis=-1, keepdims=True) + RMS_EPS)
    return (y * g).astype(x.dtype)


def layer_norm(x, g, b):
    xf = x.astype(jnp.float32)
    mu = jnp.mean(xf, axis=-1, keepdims=True)
    var = jnp.mean(jnp.square(xf - mu), axis=-1, keepdims=True)
    return ((xf - mu) * lax.rsqrt(var + LN_EPS) * g + b).astype(x.dtype)


def modulate(x, shift, scale):
    return x * (1 + scale) + shift


def rope_1d(x, ang):
    d2 = ang.shape[-1]
    cos = jnp.cos(ang)[:, None, :]
    sin = jnp.sin(ang)[:, None, :]
    xf = x.astype(jnp.float32)
    x1, x2 = xf[..., :d2], xf[..., d2:]
    return jnp.concatenate([x1 * cos - x2 * sin, x2 * cos + x1 * sin], axis=-1).astype(x.dtype)


def rope_2d(x, row, col):
    half = x.shape[-1] // 2
    inv = ROPE_BASE ** (-(jnp.arange(0, half, 2, dtype=jnp.float32) / half))
    return jnp.concatenate([rope_1d(x[..., :half], row[:, None] * inv),
                            rope_1d(x[..., half:], col[:, None] * inv)], axis=-1)


def blocked_attention(q, k, v, scale):
    B, L, Hk, G, Dq = q.shape
    nb = L // Q_BLOCK
    qb = jnp.moveaxis(q.reshape(B, nb, Q_BLOCK, Hk, G, Dq), 1, 0)

    def one_block(qblk):
        s = jnp.einsum('bqhgd,bkhd->bhgqk', qblk, k, preferred_element_type=jnp.float32) * scale
        p = jax.nn.softmax(s, axis=-1)
        return jnp.einsum('bhgqk,bkhd->bqhgd', p.astype(v.dtype), v)

    o = lax.map(one_block, qb)
    return jnp.moveaxis(o, 0, 1).reshape(B, L, Hk * G * v.shape[-1])


def mixer_inputs(h, p_attn, row, col):
    w_in, g_q_a, w_q_b, g_kv_a, w_kv_b, g_gq, g_gk = p_attn
    B, L, _ = h.shape
    z = h @ w_in
    q_a, kv_a, k_rope, gq, gk, gv = jnp.split(z, IN_SPLITS, axis=-1)
    q = (rms_norm(q_a, g_q_a) @ w_q_b).reshape(B, L, MLA_HEADS, MLA_QK)
    q_nope, q_rope = q[..., :MLA_NOPE], q[..., MLA_NOPE:]
    kv = (rms_norm(kv_a, g_kv_a) @ w_kv_b).reshape(B, L, MLA_HEADS, MLA_NOPE + MLA_V)
    k_nope, v_mla = kv[..., :MLA_NOPE], kv[..., MLA_NOPE:]
    k_rope = k_rope.reshape(B, L, 1, MLA_ROPE)
    gq = rms_norm(gq.reshape(B, L, GQA_HEADS, HEAD_DIM), g_gq)
    gk = rms_norm(gk.reshape(B, L, GQA_KV_HEADS, HEAD_DIM), g_gk)
    gv = gv.reshape(B, L, GQA_KV_HEADS, HEAD_DIM)
    if row is not None:
        q_rope = rope_2d(q_rope, row, col)
        k_rope = rope_2d(k_rope, row, col)
        gq = rope_2d(gq, row, col)
        gk = rope_2d(gk, row, col)
    q_mla = jnp.concatenate([q_nope, q_rope], axis=-1)[:, :, :, None, :]
    k_mla = jnp.concatenate([k_nope, jnp.broadcast_to(k_rope, (B, L, MLA_HEADS, MLA_ROPE))], axis=-1)
    q_gqa = gq.reshape(B, L, GQA_KV_HEADS, GQA_GROUP, HEAD_DIM)
    return q_mla, k_mla, v_mla, q_gqa, gk, gv


def moe_ffn(h, w_grp, b_grp, w_exp, b_exp, w_gate, w_up, w_down):
    T, D = h.shape
    lg = (h @ w_grp + b_grp).astype(jnp.float32)
    pg = jax.nn.softmax(lg, axis=-1)
    g_star = jnp.argmax(lg, axis=-1)
    p_star = jnp.take_along_axis(pg, g_star[:, None], axis=-1)
    le = (h @ w_exp + b_exp).astype(jnp.float32).reshape(T, N_GROUPS, EXPERTS_PER_GROUP)
    le_g = jnp.take_along_axis(le, g_star[:, None, None], axis=1)[:, 0]
    top_v, top_i = lax.top_k(le_g, TOP_K)
    weights = (p_star * jax.nn.softmax(top_v, axis=-1)).astype(h.dtype)
    expert_id = g_star[:, None] * EXPERTS_PER_GROUP + top_i

    A = T * TOP_K
    flat_e = expert_id.reshape(A)
    order = jnp.argsort(flat_e)
    sorted_e = flat_e[order]
    counts = jnp.bincount(flat_e, length=N_EXPERTS)
    padded = (counts + MOE_BLOCK - 1) // MOE_BLOCK * MOE_BLOCK
    pad_end = jnp.cumsum(padded)
    pad_start = pad_end - padded
    start = jnp.cumsum(counts) - counts
    dest_sorted = pad_start[sorted_e] + jnp.arange(A) - start[sorted_e]
    n_blocks = -(-A // MOE_BLOCK) + N_EXPERTS
    slot_tok = jnp.zeros((n_blocks * MOE_BLOCK,), jnp.int32).at[dest_sorted].set(order // TOP_K)
    block_e = jnp.minimum(jnp.searchsorted(pad_end, jnp.arange(n_blocks) * MOE_BLOCK, side='right'),
                          N_EXPERTS - 1)
    xb = h[slot_tok].reshape(n_blocks, MOE_BLOCK, D)

    def expert_block(args):
        xblk, e = args
        return (jax.nn.silu(xblk @ w_gate[e]) * (xblk @ w_up[e])) @ w_down[e]

    yb = lax.map(expert_block, (xb, block_e)).reshape(n_blocks * MOE_BLOCK, D)
    dest = jnp.zeros((A,), jnp.int32).at[order].set(dest_sorted)
    y = yb[dest].reshape(T, TOP_K, D)
    return jnp.einsum('tk,tkd->td', weights, y)


def setup_inputs(seed: int = 0) -> dict:
    key = jax.random.key(seed)
    ks = jax.random.split(key, 26)
    f32 = jnp.float32
    L, D = DEPTH, D_MODEL

    def nrm(k, shape, scale):
        return jax.random.normal(k, shape, f32) * scale

    return {
        "x": nrm(ks[0], (BATCH, SEQ, D), 1.0),
        "c": nrm(ks[1], (BATCH, D), 1.0),
        "ctx": nrm(ks[2], (BATCH, CTX_LEN, D), 1.0),
        "c_ctx": nrm(ks[3], (D,), 1.0),
        "w_ada": nrm(ks[4], (L, D, 6 * D), D ** -0.5),
        "b_ada": nrm(ks[5], (L, 6 * D), 0.02),
        "w_in": nrm(ks[6], (L, D, IN_COLS), D ** -0.5),
        "g_q_a": 1.0 + nrm(ks[7], (L, MLA_Q_LORA), 0.02),
        "w_q_b": nrm(ks[8], (L, MLA_Q_LORA, MLA_HEADS * MLA_QK), MLA_Q_LORA ** -0.5),
        "g_kv_a": 1.0 + nrm(ks[9], (L, MLA_KV_LORA), 0.02),
        "w_kv_b": nrm(ks[10], (L, MLA_KV_LORA, MLA_HEADS * (MLA_NOPE + MLA_V)), MLA_KV_LORA ** -0.5),
        "g_gqa_q": 1.0 + nrm(ks[11], (L, HEAD_DIM), 0.02),
        "g_gqa_k": 1.0 + nrm(ks[12], (L, HEAD_DIM), 0.02),
        "w_o": nrm(ks[13], (L, MIX_WIDTH, D), MIX_WIDTH ** -0.5 * BETA),
        "ln1_g": 1.0 + nrm(ks[14], (L, D), 0.02),
        "ln1_b": nrm(ks[15], (L, D), 0.02),
        "w_router_grp": nrm(ks[16], (L, D, N_GROUPS), D ** -0.5),
        "b_router_grp": nrm(ks[17], (L, N_GROUPS), 0.01),
        "w_router_exp": nrm(ks[18], (L, D, N_EXPERTS), D ** -0.5),
        "b_router_exp": nrm(ks[19], (L, N_EXPERTS), 0.01),
        "w_gate": nrm(ks[20], (L, N_EXPERTS, D, D_FF_EXPERT), D ** -0.5),
        "w_up": nrm(ks[21], (L, N_EXPERTS, D, D_FF_EXPERT), D ** -0.5),
        "w_down": nrm(ks[22], (L, N_EXPERTS, D_FF_EXPERT, D), D_FF_EXPERT ** -0.5 * BETA),
        "ln2_g": 1.0 + nrm(ks[23], (L, D), 0.02),
        "ln2_b": nrm(ks[24], (L, D), 0.02),
    }


def reference(x, c, ctx, c_ctx, w_ada, b_ada, w_in, g_q_a, w_q_b, g_kv_a, w_kv_b, g_gqa_q, g_gqa_k,
              w_o, ln1_g, ln1_b, w_router_grp, b_router_grp, w_router_exp, b_router_exp,
              w_gate, w_up, w_down, ln2_g, ln2_b):
    B, S, D = x.shape
    C = ctx.shape[1]
    rows = S // GRID_W
    row = jnp.repeat(jnp.arange(rows, dtype=jnp.float32), GRID_W)
    col = jnp.tile(jnp.arange(GRID_W, dtype=jnp.float32), rows)
    silu_c = jax.nn.silu(c)
    silu_cc = jax.nn.silu(c_ctx)
    xc = ctx
    for l in range(DEPTH):
        last = l == DEPTH - 1
        sh_a, sc_a, ga, sh_f, sc_f, gf = jnp.split((silu_c @ w_ada[l] + b_ada[l])[:, None, :], 6, axis=-1)
        csh_a, csc_a, cga, csh_f, csc_f, cgf = jnp.split(silu_cc @ w_ada[l] + b_ada[l], 6, axis=-1)
        p_attn = (w_in[l], g_q_a[l], w_q_b[l], g_kv_a[l], w_kv_b[l], g_gqa_q[l], g_gqa_k[l])

        qm, km, vm, qg, kg, vg = mixer_inputs(modulate(x, sh_a, sc_a), p_attn, row, col)
        qmc, kmc, vmc, qgc, kgc, vgc = mixer_inputs(modulate(xc, csh_a, csc_a), p_attn, None, None)
        o = jnp.concatenate([
            blocked_attention(qm, jnp.concatenate([kmc, km], axis=1),
                              jnp.concatenate([vmc, vm], axis=1), MLA_SCALE),
            blocked_attention(qg, jnp.concatenate([kgc, kg], axis=1),
                              jnp.concatenate([vgc, vg], axis=1), GQA_SCALE)], axis=-1)
        x = layer_norm(ALPHA * x + ga * (o @ w_o[l]), ln1_g[l], ln1_b[l])
        if not last:
            oc = jnp.concatenate([blocked_attention(qmc, kmc, vmc, MLA_SCALE),
                                  blocked_attention(qgc, kgc, vgc, GQA_SCALE)], axis=-1)
            xc = layer_norm(ALPHA * xc + cga * (oc @ w_o[l]), ln1_g[l], ln1_b[l])

        p_moe = (w_router_grp[l], b_router_grp[l], w_router_exp[l], b_router_exp[l],
                 w_gate[l], w_up[l], w_down[l])
        h = modulate(x, sh_f, sc_f).reshape(B * S, D)
        if last:
            y = moe_ffn(h, *p_moe).reshape(B, S, D)
        else:
            hc = modulate(xc, csh_f, csc_f).reshape(B * C, D)
            y_all = moe_ffn(jnp.concatenate([h, hc], axis=0), *p_moe)
            y = y_all[:B * S].reshape(B, S, D)
            xc = layer_norm(ALPHA * xc + cgf * y_all[B * S:].reshape(B, C, D), ln2_g[l], ln2_b[l])
        x = layer_norm(ALPHA * x + gf * y, ln2_g[l], ln2_b[l])
    return x
```

```python
import functools

import jax
import jax.numpy as jnp
from jax import lax
from jax.experimental import pallas as pl
from jax.experimental.pallas import tpu as pltpu

GRID_W = 64
ROPE_BASE = 10000.0
RMS_EPS = 1e-6
LN_EPS = 1e-5

MLA_HEADS = 8
MLA_Q_LORA = 384
MLA_KV_LORA = 256
MLA_NOPE = 64
MLA_ROPE = 32
MLA_V = 64
MLA_QK = MLA_NOPE + MLA_ROPE
MLA_SCALE = MLA_QK ** -0.5

HEAD_DIM = 64
GQA_HEADS = 8
GQA_KV_HEADS = 2
GQA_SCALE = HEAD_DIM ** -0.5

N_GROUPS = 4
EXPERTS_PER_GROUP = 8
N_EXPERTS = N_GROUPS * EXPERTS_PER_GROUP
TOP_K = 2

LANES = 128
MOE_ROWS = 256
VMEM_LIMIT = 56 * 1024 * 1024

_QA0, _KVA0, _KR0, _GQ0, _GK0, _GV0, _INW = 0, 384, 640, 768, 1280, 1408, 1536

_f32 = jnp.float32
_bf16 = jnp.bfloat16


def _cparams(sem):
    return pltpu.CompilerParams(dimension_semantics=sem, vmem_limit_bytes=VMEM_LIMIT)


def _sigmoid(v):
    return 1.0 / (1.0 + jnp.exp(-v))


def _layer_norm(v, g, b):
    mu = jnp.mean(v, axis=-1, keepdims=True)
    d = v - mu
    var = jnp.mean(d * d, axis=-1, keepdims=True)
    return d * lax.rsqrt(var + LN_EPS) * g + b


def _rot_half(y, dist):
    lane = lax.broadcasted_iota(jnp.int32, y.shape, y.ndim - 1)
    n = y.shape[-1]
    fwd = pltpu.roll(y, n - dist, y.ndim - 1)
    bwd = pltpu.roll(y, dist, y.ndim - 1)
    return jnp.where((lane & dist) == 0, fwd, bwd)


def _adaln_kernel(c_ref, w_ref, b_ref, o_ref):
    cc = c_ref[...]
    s = (cc * _sigmoid(cc)).astype(_bf16)
    o_ref[0] = jnp.dot(s, w_ref[0].astype(_bf16), preferred_element_type=_f32) + b_ref[0]


def _adaln(cc, w_ada, b_ada):
    L, D, N = w_ada.shape
    R = cc.shape[0]
    tn = 1536 if N % 1536 == 0 else N
    return pl.pallas_call(
        _adaln_kernel,
        grid=(L, N // tn),
        in_specs=[pl.BlockSpec((R, D), lambda l, j: (0, 0)),
                  pl.BlockSpec((1, D, tn), lambda l, j: (l, 0, j)),
                  pl.BlockSpec((1, 1, tn), lambda l, j: (l, 0, j))],
        out_specs=pl.BlockSpec((1, R, tn), lambda l, j: (l, 0, j)),
        out_shape=jax.ShapeDtypeStruct((L, R, N), _f32),
        compiler_params=_cparams(("arbitrary", "arbitrary")),
    )(cc, w_ada, b_ada.reshape(L, 1, N))


def _inproj_kernel(x_ref, mod_ref, win_ref, wqb_ref, wkvb_ref, gqa_ref, gkva_ref, hs_ref,
                   aqm_ref, bqm_ref, akm_ref, bkm_ref, aqg_ref, bqg_ref, akg_ref, bkg_ref,
                   qm_ref, km_ref, vm_ref, qg_ref, kg_ref, vg_ref):
    x = x_ref[0]
    h = (x * (1.0 + mod_ref[1:2, :]) + mod_ref[0:1, :]).astype(_bf16)
    z = jnp.dot(h, win_ref[...], preferred_element_type=_f32)

    q_a = z[:, _QA0:_KVA0]
    qn = q_a * lax.rsqrt(jnp.mean(q_a * q_a, axis=-1, keepdims=True) + RMS_EPS) * gqa_ref[...]
    q = jnp.dot(qn.astype(_bf16), wqb_ref[...], preferred_element_type=_f32)
    kv_a = z[:, _KVA0:_KR0]
    kvn = kv_a * lax.rsqrt(jnp.mean(kv_a * kv_a, axis=-1, keepdims=True) + RMS_EPS) * gkva_ref[...]
    kv = jnp.dot(kvn.astype(_bf16), wkvb_ref[...], preferred_element_type=_f32)
    aq, bq = aqm_ref[...], bqm_ref[...]
    for hd in range(MLA_HEADS):
        qh = q[:, hd * LANES:(hd + 1) * LANES]
        qm_ref[0, hd] = (qh * aq + _rot_half(qh, 8) * bq).astype(_bf16)
    kr = z[:, _KR0:_GQ0]
    kr = kr * akm_ref[...] + _rot_half(kr, 8) * bkm_ref[...]
    for hd in range(MLA_HEADS):
        km_ref[0, hd] = (kv[:, hd * LANES:(hd + 1) * LANES] + kr).astype(_bf16)
    v0 = MLA_HEADS * LANES
    for p in range(MLA_HEADS // 2):
        vm_ref[0, p] = kv[:, v0 + p * LANES:v0 + (p + 1) * LANES].astype(_bf16)

    def head_rs(t, hs):
        sq = t * t
        hi = sq.astype(_bf16)
        lo = (sq - hi.astype(_f32)).astype(_bf16)
        ssum = (jnp.dot(hi, hs, preferred_element_type=_f32)
                + jnp.dot(lo, hs, preferred_element_type=_f32))
        return lax.rsqrt(ssum * (1.0 / HEAD_DIM) + RMS_EPS)

    gq = z[:, _GQ0:_GK0]
    yq = gq * head_rs(gq, hs_ref[...])
    aqg, bqg = aqg_ref[...], bqg_ref[...]
    for j in range(GQA_HEADS // 2):
        yj = yq[:, j * LANES:(j + 1) * LANES]
        qg_ref[0, j] = (yj * aqg + _rot_half(yj, 16) * bqg).astype(_bf16)
    gk = z[:, _GK0:_GV0]
    yk = gk * head_rs(gk, hs_ref[0:LANES, 0:LANES])
    kg_ref[0] = (yk * akg_ref[...] + _rot_half(yk, 16) * bkg_ref[...]).astype(_bf16)
    vg_ref[0] = z[:, _GV0:_INW].astype(_bf16)


def _inproj(xa, mod, win, wqb, wkvb, gqa, gkva, hs, tabs_m, tabs_g, tm, n_ctx_tiles):
    B, T, D = xa.shape
    nt = T // tm
    full = lambda a: pl.BlockSpec(a.shape, lambda b, i: (0,) * a.ndim)
    tab = pl.BlockSpec((tm, LANES), lambda b, i: (i, 0))
    out4 = lambda n: pl.BlockSpec((1, n, tm, LANES), lambda b, i: (b, 0, i, 0))
    out3 = pl.BlockSpec((1, tm, LANES), lambda b, i: (b, i, 0))
    sds = lambda *s: jax.ShapeDtypeStruct(s, _bf16)
    return pl.pallas_call(
        _inproj_kernel,
        grid=(B, nt),
        in_specs=[pl.BlockSpec((1, tm, D), lambda b, i: (b, i, 0)),
                  pl.BlockSpec((None, None, 6, D), lambda b, i: (b, jnp.where(i >= n_ctx_tiles, 1, 0), 0, 0)),
                  full(win), full(wqb), full(wkvb), full(gqa), full(gkva), full(hs),
                  tab, tab, tab, tab, tab, tab, tab, tab],
        out_specs=[out4(8), out4(8), out4(4), out4(4), out3, out3],
        out_shape=[sds(B, 8, T, LANES), sds(B, 8, T, LANES), sds(B, 4, T, LANES),
                   sds(B, 4, T, LANES), sds(B, T, LANES), sds(B, T, LANES)],
        compiler_params=_cparams(("arbitrary", "arbitrary")),
    )(xa, mod, win, wqb, wkvb, gqa, gkva, hs, *tabs_m, *tabs_g)


def _softmax_pv(q, k, v):
    s = lax.dot_general(q, k, (((1,), (1,)), ((), ())), preferred_element_type=_f32)
    m = jnp.max(s, axis=-1, keepdims=True)
    e = jnp.exp(s - m)
    l = jnp.sum(e, axis=-1, keepdims=True)
    pv = jnp.dot(e.astype(_bf16), v, preferred_element_type=_f32)
    return pv * (1.0 / l)


def _attn_kernel(qm_ref, km_ref, vm_ref, qg_ref, kg_ref, vg_ref, o_ref, *, n_ctx, n_ctx_tiles):
    tq = qm_ref.shape[2]
    n_all = km_ref.shape[2]
    lane = lax.broadcasted_iota(jnp.int32, (tq, LANES), 1)
    low = lane < (LANES // 2)

    def run(klen):
        def mla_pair(p, carry):
            v = vm_ref[0, p, 0:klen, :]
            o0 = _softmax_pv(qm_ref[0, 2 * p], km_ref[0, 2 * p, 0:klen, :], v)
            o1 = _softmax_pv(qm_ref[0, 2 * p + 1], km_ref[0, 2 * p + 1, 0:klen, :], v)
            o_ref[0, p] = jnp.where(low, o0, o1).astype(_bf16)
            return carry

        lax.fori_loop(0, MLA_HEADS // 2, mla_pair, 0)

        def gqa_slab(j, carry):
            qs = qg_ref[0, j]
            k = kg_ref[0, 0:klen, :]
            v = vg_ref[0, 0:klen, :]
            o0 = _softmax_pv(jnp.where(low, qs, jnp.zeros_like(qs)), k, v)
            o1 = _softmax_pv(jnp.where(low, jnp.zeros_like(qs), qs), k, v)
            o_ref[0, MLA_HEADS // 2 + j] = jnp.where(low, o0, o1).astype(_bf16)
            return carry

        lax.fori_loop(0, GQA_HEADS // 2, gqa_slab, 0)

    i = pl.program_id(1)

    @pl.when(i < n_ctx_tiles)
    def _():
        run(n_ctx)

    @pl.when(i >= n_ctx_tiles)
    def _():
        run(n_all)


def _attention(qm, km, vm, qg, kg, vg, tq, n_ctx):
    B, _, T, _ = qm.shape
    qspec = lambda n: pl.BlockSpec((1, n, tq, LANES), lambda b, i: (b, 0, i, 0))
    kspec = lambda n: pl.BlockSpec((1, n, T, LANES), lambda b, i: (b, 0, 0, 0))
    k3 = pl.BlockSpec((1, T, LANES), lambda b, i: (b, 0, 0))
    return pl.pallas_call(
        functools.partial(_attn_kernel, n_ctx=n_ctx, n_ctx_tiles=n_ctx // tq),
        grid=(B, T // tq),
        in_specs=[qspec(8), kspec(8), kspec(4), qspec(4), k3, k3],
        out_specs=qspec(8),
        out_shape=jax.ShapeDtypeStruct((B, 8, T, LANES), _bf16),
        compiler_params=_cparams(("arbitrary", "arbitrary")),
    )(qm, km, vm, qg, kg, vg)


def _wo_ln_kernel(x_ref, o_ref, mod_ref, wo_ref, g_ref, b_ref, out_ref, *, alpha):
    o = jnp.concatenate([o_ref[0, p] for p in range(o_ref.shape[1])], axis=-1)
    mix = jnp.dot(o, wo_ref[...], preferred_element_type=_f32)
    out_ref[0] = _layer_norm(alpha * x_ref[0] + mod_ref[2:3, :] * mix, g_ref[...], b_ref[...])


def _wo_ln(xa, o, mod, wo, g, b, tm, n_ctx_tiles, alpha):
    B, T, D = xa.shape
    full = lambda a: pl.BlockSpec(a.shape, lambda bb, i: (0,) * a.ndim)
    return pl.pallas_call(
        functools.partial(_wo_ln_kernel, alpha=alpha),
        grid=(B, T // tm),
        in_specs=[pl.BlockSpec((1, tm, D), lambda bb, i: (bb, i, 0)),
                  pl.BlockSpec((1, 8, tm, LANES), lambda bb, i: (bb, 0, i, 0)),
                  pl.BlockSpec((None, None, 6, D), lambda bb, i: (bb, jnp.where(i >= n_ctx_tiles, 1, 0), 0, 0)),
                  full(wo), full(g), full(b)],
        out_specs=pl.BlockSpec((1, tm, D), lambda bb, i: (bb, i, 0)),
        out_shape=jax.ShapeDtypeStruct((B, T, D), _f32),
        compiler_params=_cparams(("arbitrary", "arbitrary")),
    )(xa, o, mod, wo, g, b)


def _router_kernel(x_ref, mod_ref, wr_ref, br_ref, h_ref, info_ref, cnt_ref, carry_ref):
    tm = x_ref.shape[1]
    first = (pl.program_id(0) == 0) & (pl.program_id(1) == 0)

    @pl.when(first)
    def _():
        carry_ref[...] = jnp.zeros_like(carry_ref)

    h = x_ref[0] * (1.0 + mod_ref[4:5, :]) + mod_ref[3:4, :]
    h_ref[0] = h
    logits = jnp.dot(h.astype(_bf16), wr_ref[...], preferred_element_type=_f32) + br_ref[...]
    lane_i = lax.broadcasted_iota(jnp.int32, (tm, LANES), 1)
    lane = lane_i.astype(_f32)
    neg = jnp.float32(-jnp.inf)
    big = jnp.float32(LANES)

    def arg_first_max(vals):
        mx = jnp.max(vals, axis=-1, keepdims=True)
        idx = jnp.min(jnp.where(vals == mx, lane, big), axis=-1, keepdims=True)
        return mx, idx

    is_grp = (lane_i >= N_EXPERTS) & (lane_i < N_EXPERTS + N_GROUPS)
    lg = jnp.where(is_grp, logits, neg)
    gmax, gidx = arg_first_max(lg)
    p_star = 1.0 / jnp.sum(jnp.exp(lg - gmax), axis=-1, keepdims=True)
    g_star = gidx - N_EXPERTS
    in_grp = (lane_i < N_EXPERTS) & ((lane_i >> 3).astype(_f32) == g_star)
    le = jnp.where(in_grp, logits, neg)
    v1, e1 = arg_first_max(le)
    v2, e2 = arg_first_max(jnp.where(lane == e1, neg, le))
    t = jnp.exp(v2 - v1)
    w1 = p_star / (1.0 + t)
    w2 = p_star * t / (1.0 + t)

    onehot = ((lane == e1) | (lane == e2)).astype(_f32)
    r = lax.broadcasted_iota(jnp.int32, (tm, tm), 0)
    c = lax.broadcasted_iota(jnp.int32, (tm, tm), 1)
    tri = (c < r).astype(_bf16)
    prefix = jnp.dot(tri, onehot.astype(_bf16), preferred_element_type=_f32) + carry_ref[...]
    r1 = jnp.sum(jnp.where(lane == e1, prefix, 0.0), axis=-1, keepdims=True)
    r2 = jnp.sum(jnp.where(lane == e2, prefix, 0.0), axis=-1, keepdims=True)
    carry_ref[...] += jnp.sum(onehot, axis=0, keepdims=True)
    cnt_ref[...] = carry_ref[...]

    l8 = lax.broadcasted_iota(jnp.int32, (tm, 8), 1)
    cols = [e1, e2, w1, w2, r1, r2]
    info = jnp.zeros((tm, 8), _f32)
    for k, col in enumerate(cols):
        info = jnp.where(l8 == k, col, info)
    info_ref[0] = info


def _router(x1, mod, wr, br, tm, n_ctx_tiles):
    B, T, D = x1.shape
    full = lambda a: pl.BlockSpec(a.shape, lambda bb, i: (0,) * a.ndim)
    return pl.pallas_call(
        _router_kernel,
        grid=(B, T // tm),
        in_specs=[pl.BlockSpec((1, tm, D), lambda bb, i: (bb, i, 0)),
                  pl.BlockSpec((None, None, 6, D), lambda bb, i: (bb, jnp.where(i >= n_ctx_tiles, 1, 0), 0, 0)),
                  full(wr), full(br)],
        out_specs=[pl.BlockSpec((1, tm, D), lambda bb, i: (bb, i, 0)),
                   pl.BlockSpec((1, tm, 8), lambda bb, i: (bb, i, 0)),
                   pl.BlockSpec((1, LANES), lambda bb, i: (0, 0))],
        out_shape=[jax.ShapeDtypeStruct((B, T, D), _f32),
                   jax.ShapeDtypeStruct((B, T, 8), _f32),
                   jax.ShapeDtypeStruct((1, LANES), _f32)],
        scratch_shapes=[pltpu.VMEM((1, LANES), _f32)],
        compiler_params=_cparams(("arbitrary", "arbitrary")),
    )(x1, mod, wr, br)


def _row_copy(src_ref, row, dst_ref, slot, sem):
    return pltpu.make_async_copy(src_ref.at[pl.ds(row, 1), :], dst_ref.at[pl.ds(slot, 1), :], sem)


def _expert_kernel(be_ref, nu_ref, tok_ref, h_ref, wg_ref, wu_ref, wd_ref, y_ref,
                   xbuf, wg_bf, wu_bf, wd_bf, sem):
    i = pl.program_id(0)
    rows = xbuf.shape[0]

    @pl.when(i < nu_ref[0])
    def _():
        def start(r, carry):
            _row_copy(h_ref, tok_ref[0, 0, r], xbuf, r, sem).start()
            return carry

        lax.fori_loop(0, rows, start, 0)

        new_expert = jnp.logical_or(i == 0, be_ref[i] != be_ref[jnp.maximum(i - 1, 0)])

        @pl.when(new_expert)
        def _():
            wg_bf[...] = wg_ref[0].astype(_bf16)
            wu_bf[...] = wu_ref[0].astype(_bf16)
            wd_bf[...] = wd_ref[0].astype(_bf16)

        def wait(r, carry):
            _row_copy(h_ref, 0, xbuf, r, sem).wait()
            return carry

        lax.fori_loop(0, rows, wait, 0)

        xb = xbuf[...].astype(_bf16)
        g = jnp.dot(xb, wg_bf[...], preferred_element_type=_f32)
        u = jnp.dot(xb, wu_bf[...], preferred_element_type=_f32)
        a = (g * _sigmoid(g) * u).astype(_bf16)
        y_ref[...] = jnp.dot(a, wd_bf[...], preferred_element_type=_f32)

    @pl.when(i >= nu_ref[0])
    def _():
        y_ref[...] = jnp.zeros_like(y_ref)


def _experts(h2d, slot_tok, block_e, n_used, w_gate, w_up, w_down, rows):
    n_blocks = slot_tok.shape[0]
    _, D, F = w_gate.shape
    grid_spec = pltpu.PrefetchScalarGridSpec(
        num_scalar_prefetch=2,
        grid=(n_blocks,),
        in_specs=[pl.BlockSpec((1, 1, rows), lambda i, be, nu: (i, 0, 0), memory_space=pltpu.SMEM),
                  pl.BlockSpec(memory_space=pl.ANY),
                  pl.BlockSpec((1, D, F), lambda i, be, nu: (be[i], 0, 0)),
                  pl.BlockSpec((1, D, F), lambda i, be, nu: (be[i], 0, 0)),
                  pl.BlockSpec((1, F, D), lambda i, be, nu: (be[i], 0, 0))],
        out_specs=pl.BlockSpec((rows, D), lambda i, be, nu: (i, 0)),
        scratch_shapes=[pltpu.VMEM((rows, D), _f32),
                        pltpu.VMEM((D, F), _bf16), pltpu.VMEM((D, F), _bf16), pltpu.VMEM((F, D), _bf16),
                        pltpu.SemaphoreType.DMA(())],
    )
    return pl.pallas_call(
        _expert_kernel,
        grid_spec=grid_spec,
        out_shape=jax.ShapeDtypeStruct((n_blocks * rows, D), _f32),
        compiler_params=_cparams(("arbitrary",)),
    )(block_e, n_used, slot_tok, h2d, w_gate, w_up, w_down)


def _combine_kernel(dest_ref, x_ref, mod_ref, info_ref, yb_ref, g_ref, b_ref, out_ref, ybuf, sem, *, alpha):
    tm = x_ref.shape[1]

    def start(r, carry):
        _row_copy(yb_ref, dest_ref[0, 0, r], ybuf, r, sem).start()
        return carry

    lax.fori_loop(0, 2 * tm, start, 0)

    def wait(r, carry):
        _row_copy(yb_ref, 0, ybuf, r, sem).wait()
        return carry

    lax.fori_loop(0, 2 * tm, wait, 0)

    info = info_ref[0]
    y = info[:, 2:3] * ybuf[0:tm, :] + info[:, 3:4] * ybuf[tm:2 * tm, :]
    out_ref[0] = _layer_norm(alpha * x_ref[0] + mod_ref[5:6, :] * y, g_ref[...], b_ref[...])


def _combine_ln(x1, mod, info, dest, yb, g, b, tm, n_ctx_tiles, alpha):
    B, T, D = x1.shape
    nt = T // tm
    full = lambda a: pl.BlockSpec(a.shape, lambda bb, i: (0,) * a.ndim)
    return pl.pallas_call(
        functools.partial(_combine_kernel, alpha=alpha),
        grid=(B, nt),
        in_specs=[pl.BlockSpec((1, 1, 2 * tm), lambda bb, i: (bb * nt + i, 0, 0), memory_space=pltpu.SMEM),
                  pl.BlockSpec((1, tm, D), lambda bb, i: (bb, i, 0)),
                  pl.BlockSpec((None, None, 6, D), lambda bb, i: (bb, jnp.where(i >= n_ctx_tiles, 1, 0), 0, 0)),
                  pl.BlockSpec((1, tm, 8), lambda bb, i: (bb, i, 0)),
                  pl.BlockSpec(memory_space=pl.ANY),
                  full(g), full(b)],
        out_specs=pl.BlockSpec((1, tm, D), lambda bb, i: (bb, i, 0)),
        out_shape=jax.ShapeDtypeStruct((B, T, D), _f32),
        scratch_shapes=[pltpu.VMEM((2 * tm, D), _f32), pltpu.SemaphoreType.DMA(())],
        compiler_params=_cparams(("arbitrary", "arbitrary")),
    )(dest, x1, mod, info, yb, g, b)


def _prep_weights(w_in, w_q_b, w_kv_b, w_o, w_router_grp, b_router_grp, w_router_exp, b_router_exp):
    L, D, _ = w_in.shape
    zeros = lambda *s: jnp.zeros((L,) + s, _f32)
    kr = jnp.concatenate([zeros(D, 64), w_in[:, :, 640:672], zeros(D, 32)], axis=-1)
    gq = w_in[:, :, 672:1184].reshape(L, D, GQA_HEADS, HEAD_DIM)
    gq = jnp.stack([gq[:, :, :4], gq[:, :, 4:]], axis=3).reshape(L, D, 512)
    win = jnp.concatenate([w_in[:, :, 0:640], kr, gq, w_in[:, :, 1184:1440]], axis=-1).astype(_bf16)

    qb = w_q_b.reshape(L, MLA_Q_LORA, MLA_HEADS, MLA_QK)
    wqb = jnp.concatenate([qb, zeros(MLA_Q_LORA, MLA_HEADS, LANES - MLA_QK)], axis=-1)
    wqb = wqb.reshape(L, MLA_Q_LORA, MLA_HEADS * LANES).astype(_bf16)

    kvb = w_kv_b.reshape(L, MLA_KV_LORA, MLA_HEADS, MLA_NOPE + MLA_V)
    kpart = jnp.concatenate([kvb[..., :MLA_NOPE], zeros(MLA_KV_LORA, MLA_HEADS, LANES - MLA_NOPE)], axis=-1)
    vpart = kvb[..., MLA_NOPE:].reshape(L, MLA_KV_LORA, MLA_HEADS * MLA_V)
    wkvb = jnp.concatenate([kpart.reshape(L, MLA_KV_LORA, MLA_HEADS * LANES), vpart], axis=-1).astype(_bf16)

    n_mla = MLA_HEADS * MLA_V
    og = w_o[:, n_mla:, :].reshape(L, GQA_HEADS, HEAD_DIM, D)
    og = jnp.stack([og[:, :4], og[:, 4:]], axis=2).reshape(L, GQA_HEADS * HEAD_DIM, D)
    wo = jnp.concatenate([w_o[:, :n_mla, :], og], axis=1).astype(_bf16)

    pad = LANES - N_EXPERTS - N_GROUPS
    wr = jnp.concatenate([w_router_exp, w_router_grp, zeros(D, pad)], axis=-1).astype(_bf16)
    br = jnp.concatenate([b_router_exp, b_router_grp, jnp.zeros((L, pad), _f32)], axis=-1).reshape(L, 1, LANES)
    return win, wqb, wkvb, wo, wr, br


def _rope_tables(S, C, g_gqa_q, g_gqa_k):
    pos = jnp.arange(S)
    row = (pos // GRID_W).astype(_f32)
    col = (pos % GRID_W).astype(_f32)

    def angles(half):
        inv = ROPE_BASE ** (-(jnp.arange(0, half, 2, dtype=_f32) / half))
        ar, ac = row[:, None] * inv, col[:, None] * inv
        return jnp.concatenate([ar, ar, ac, ac], axis=-1)

    def cos_sin(half):
        ang = angles(half)
        sign = jnp.tile(jnp.concatenate([-jnp.ones(half // 2), jnp.ones(half // 2)]), 2).astype(_f32)
        cos = jnp.concatenate([jnp.ones((C, 2 * half), _f32), jnp.cos(ang)], axis=0)
        sin = jnp.concatenate([jnp.zeros((C, 2 * half), _f32), jnp.sin(ang) * sign], axis=0)
        return cos, sin

    T = S + C
    cm, sm = cos_sin(MLA_ROPE // 2)
    a_m = jnp.concatenate([jnp.ones((T, 64), _f32), cm, jnp.ones((T, 32), _f32)], axis=-1)
    b_m = jnp.concatenate([jnp.zeros((T, 64), _f32), sm, jnp.zeros((T, 32), _f32)], axis=-1)
    tabs_m = (a_m * MLA_SCALE, b_m * MLA_SCALE, a_m, b_m)

    cg, sg = cos_sin(HEAD_DIM // 2)
    swap = jnp.arange(HEAD_DIM) ^ (HEAD_DIM // 4)

    def gqa(g, scale):
        a = g[:, None, :] * cg[None] * scale
        b = g[:, swap][:, None, :] * sg[None] * scale
        return jnp.tile(a, (1, 1, 2)), jnp.tile(b, (1, 1, 2))

    aq, bq = gqa(g_gqa_q, GQA_SCALE)
    ak, bk = gqa(g_gqa_k, 1.0)
    return tabs_m, (aq, bq, ak, bk)


def kernel(x, c, ctx, c_ctx, w_ada, b_ada, w_in, g_q_a, w_q_b, g_kv_a, w_kv_b, g_gqa_q, g_gqa_k,
           w_o, ln1_g, ln1_b, w_router_grp, b_router_grp, w_router_exp, b_router_exp,
           w_gate, w_up, w_down, ln2_g, ln2_b):
    B, S, D = x.shape
    C = ctx.shape[1]
    L = w_in.shape[0]
    T = C + S
    alpha = (2.0 * L) ** 0.25
    tm = 256 if C % 256 == 0 else 128
    assert C % tm == 0 and S % tm == 0
    n_ctx_tiles = C // tm
    rows = MOE_ROWS

    R = -(-(B + 1) // 8) * 8
    cc = jnp.concatenate([c, c_ctx[None, :], jnp.zeros((R - B - 1, D), _f32)], axis=0)
    mods = _adaln(cc, w_ada, b_ada)
    lat = mods[:, :B].reshape(L, B, 1, 6, D)
    cx = jnp.broadcast_to(mods[:, B].reshape(L, 1, 1, 6, D), (L, B, 1, 6, D))
    mods = jnp.concatenate([cx, lat], axis=2)

    win, wqb, wkvb, wo, wr, br = _prep_weights(w_in, w_q_b, w_kv_b, w_o, w_router_grp, b_router_grp,
                                               w_router_exp, b_router_exp)
    tabs_m, tabs_g = _rope_tables(S, C, g_gqa_q, g_gqa_k)
    blk = jnp.arange(GQA_HEADS * HEAD_DIM) // HEAD_DIM
    hs = (blk[:, None] == blk[None, :]).astype(_bf16)

    n_tok = B * T
    n_assign = n_tok * TOP_K
    n_blocks = -(-n_assign // rows) + N_EXPERTS
    tok_ids = jnp.repeat(jnp.arange(n_tok, dtype=jnp.int32), TOP_K)

    xa = jnp.concatenate([ctx, x], axis=1)
    for l in range(L):
        mod = mods[l]
        qm, km, vm, qg, kg, vg = _inproj(
            xa, mod, win[l], wqb[l], wkvb[l], g_q_a[l][None, :], g_kv_a[l][None, :], hs,
            tabs_m, tuple(t[l] for t in tabs_g), tm, n_ctx_tiles)
        o = _attention(qm, km, vm, qg, kg, vg, tm, C)
        x1 = _wo_ln(xa, o, mod, wo[l], ln1_g[l][None, :], ln1_b[l][None, :], tm, n_ctx_tiles, alpha)

        h, info, counts = _router(x1, mod, wr[l], br[l], tm, n_ctx_tiles)
        cnt = counts[0, :N_EXPERTS].astype(jnp.int32)
        padded = (cnt + rows - 1) // rows * rows
        pad_end = jnp.cumsum(padded)
        pad_start = pad_end - padded
        eid = info[..., 0:2].astype(jnp.int32).reshape(n_tok, TOP_K)
        rank = info[..., 4:6].astype(jnp.int32).reshape(n_tok, TOP_K)
        dest = pad_start[eid] + rank
        slot_tok = jnp.zeros((n_blocks * rows,), jnp.int32).at[dest.reshape(-1)].set(tok_ids)
        block_e = jnp.minimum(jnp.searchsorted(pad_end, jnp.arange(n_blocks, dtype=jnp.int32) * rows,
                                               side='right'), N_EXPERTS - 1).astype(jnp.int32)
        n_used = (pad_end[-1:] // rows).astype(jnp.int32)

        yb = _experts(h.reshape(n_tok, D), slot_tok.reshape(n_blocks, 1, rows), block_e, n_used,
                      w_gate[l], w_up[l], w_down[l], rows)
        dest_t = dest.reshape(n_tok // tm, tm, TOP_K).transpose(0, 2, 1).reshape(n_tok // tm, 1, TOP_K * tm)
        xa = _combine_ln(x1, mod, info, dest_t, yb, ln2_g[l][None, :], ln2_b[l][None, :],
                         tm, n_ctx_tiles, alpha)
    return xa[:, C:, :]
```

```python
import functools

import jax
import jax.numpy as jnp
from jax import lax
from jax.experimental import pallas as pl
from jax.experimental.pallas import tpu as pltpu

GRID_W = 64
ROPE_BASE = 10000.0
RMS_EPS = 1e-6
LN_EPS = 1e-5

MLA_HEADS = 8
MLA_Q_LORA = 384
MLA_KV_LORA = 256
MLA_NOPE = 64
MLA_ROPE = 32
MLA_V = 64
MLA_QK = MLA_NOPE + MLA_ROPE
MLA_SCALE = MLA_QK ** -0.5

HEAD_DIM = 64
GQA_HEADS = 8
GQA_KV_HEADS = 2
GQA_SCALE = HEAD_DIM ** -0.5

N_GROUPS = 4
EXPERTS_PER_GROUP = 8
N_EXPERTS = N_GROUPS * EXPERTS_PER_GROUP
TOP_K = 2

LANES = 128
MOE_ROWS = 256
VMEM_LIMIT = 56 * 1024 * 1024

_QA0, _KVA0, _KR0, _GQ0, _GK0, _GV0, _INW = 0, 384, 640, 768, 1280, 1408, 1536

_f32 = jnp.float32
_bf16 = jnp.bfloat16


def _cparams(sem):
    return pltpu.CompilerParams(dimension_semantics=sem, vmem_limit_bytes=VMEM_LIMIT)


def _sigmoid(v):
    return 1.0 / (1.0 + jnp.exp(-v))


def _layer_norm(v, g, b):
    mu = jnp.mean(v, axis=-1, keepdims=True)
    d = v - mu
    var = jnp.mean(d * d, axis=-1, keepdims=True)
    return d * lax.rsqrt(var + LN_EPS) * g + b


def _rot_half(y, dist):
    lane = lax.broadcasted_iota(jnp.int32, y.shape, y.ndim - 1)
    n = y.shape[-1]
    fwd = pltpu.roll(y, n - dist, y.ndim - 1)
    bwd = pltpu.roll(y, dist, y.ndim - 1)
    return jnp.where((lane & dist) == 0, fwd, bwd)


def _adaln_kernel(c_ref, w_ref, b_ref, o_ref):
    cc = c_ref[...]
    s = (cc * _sigmoid(cc)).astype(_bf16)
    o_ref[0] = jnp.dot(s, w_ref[0].astype(_bf16), preferred_element_type=_f32) + b_ref[0]


def _adaln(cc, w_ada, b_ada):
    L, D, N = w_ada.shape
    R = cc.shape[0]
    tn = 1536 if N % 1536 == 0 else N
    return pl.pallas_call(
        _adaln_kernel,
        grid=(L, N // tn),
        in_specs=[pl.BlockSpec((R, D), lambda l, j: (0, 0)),
                  pl.BlockSpec((1, D, tn), lambda l, j: (l, 0, j)),
                  pl.BlockSpec((1, 1, tn), lambda l, j: (l, 0, j))],
        out_specs=pl.BlockSpec((1, R, tn), lambda l, j: (l, 0, j)),
        out_shape=jax.ShapeDtypeStruct((L, R, N), _f32),
        compiler_params=_cparams(("arbitrary", "arbitrary")),
    )(cc, w_ada, b_ada.reshape(L, 1, N))


def _inproj_kernel(x_ref, mod_ref, win_ref, wqb_ref, wkvb_ref, gqa_ref, gkva_ref, hs_ref,
                   aqm_ref, bqm_ref, akm_ref, bkm_ref, aqg_ref, bqg_ref, akg_ref, bkg_ref,
                   qm_ref, km_ref, vm_ref, qg_ref, kg_ref, vg_ref):
    x = x_ref[0]
    h = (x * (1.0 + mod_ref[1:2, :]) + mod_ref[0:1, :]).astype(_bf16)
    z = jnp.dot(h, win_ref[...], preferred_element_type=_f32)

    q_a = z[:, _QA0:_KVA0]
    qn = q_a * lax.rsqrt(jnp.mean(q_a * q_a, axis=-1, keepdims=True) + RMS_EPS) * gqa_ref[...]
    q = jnp.dot(qn.astype(_bf16), wqb_ref[...], preferred_element_type=_f32)
    kv_a = z[:, _KVA0:_KR0]
    kvn = kv_a * lax.rsqrt(jnp.mean(kv_a * kv_a, axis=-1, keepdims=True) + RMS_EPS) * gkva_ref[...]
    kv = jnp.dot(kvn.astype(_bf16), wkvb_ref[...], preferred_element_type=_f32)
    aq, bq = aqm_ref[...], bqm_ref[...]
    for hd in range(MLA_HEADS):
        qh = q[:, hd * LANES:(hd + 1) * LANES]
        qm_ref[0, hd] = (qh * aq + _rot_half(qh, 8) * bq).astype(_bf16)
    kr = z[:, _KR0:_GQ0]
    kr = kr * akm_ref[...] + _rot_half(kr, 8) * bkm_ref[...]
    for hd in range(MLA_HEADS):
        km_ref[0, hd] = (kv[:, hd * LANES:(hd + 1) * LANES] + kr).astype(_bf16)
    v0 = MLA_HEADS * LANES
    for p in range(MLA_HEADS // 2):
        vm_ref[0, p] = kv[:, v0 + p * LANES:v0 + (p + 1) * LANES].astype(_bf16)

    def head_rs(t, hs):
        sq = t * t
        hi = sq.astype(_bf16)
        lo = (sq - hi.astype(_f32)).astype(_bf16)
        ssum = (jnp.dot(hi, hs, preferred_element_type=_f32)
                + jnp.dot(lo, hs, preferred_element_type=_f32))
        return lax.rsqrt(ssum * (1.0 / HEAD_DIM) + RMS_EPS)

    gq = z[:, _GQ0:_GK0]
    yq = gq * head_rs(gq, hs_ref[...])
    aqg, bqg = aqg_ref[...], bqg_ref[...]
    for j in range(GQA_HEADS // 2):
        yj = yq[:, j * LANES:(j + 1) * LANES]
        qg_ref[0, j] = (yj * aqg + _rot_half(yj, 16) * bqg).astype(_bf16)
    gk = z[:, _GK0:_GV0]
    yk = gk * head_rs(gk, hs_ref[0:LANES, 0:LANES])
    kg_ref[0] = (yk * akg_ref[...] + _rot_half(yk, 16) * bkg_ref[...]).astype(_bf16)
    vg_ref[0] = z[:, _GV0:_INW].astype(_bf16)


def _inproj(xa, mod, win, wqb, wkvb, gqa, gkva, hs, tabs_m, tabs_g, tm, n_ctx_tiles):
    B, T, D = xa.shape
    nt = T // tm
    full = lambda a: pl.BlockSpec(a.shape, lambda b, i: (0,) * a.ndim)
    tab = pl.BlockSpec((tm, LANES), lambda b, i: (i, 0))
    out4 = lambda n: pl.BlockSpec((1, n, tm, LANES), lambda b, i: (b, 0, i, 0))
    out3 = pl.BlockSpec((1, tm, LANES), lambda b, i: (b, i, 0))
    sds = lambda *s: jax.ShapeDtypeStruct(s, _bf16)
    return pl.pallas_call(
        _inproj_kernel,
        grid=(B, nt),
        in_specs=[pl.BlockSpec((1, tm, D), lambda b, i: (b, i, 0)),
                  pl.BlockSpec((None, None, 6, D), lambda b, i: (b, jnp.where(i >= n_ctx_tiles, 1, 0), 0, 0)),
                  full(win), full(wqb), full(wkvb), full(gqa), full(gkva), full(hs),
                  tab, tab, tab, tab, tab, tab, tab, tab],
        out_specs=[out4(8), out4(8), out4(4), out4(4), out3, out3],
        out_shape=[sds(B, 8, T, LANES), sds(B, 8, T, LANES), sds(B, 4, T, LANES),
                   sds(B, 4, T, LANES), sds(B, T, LANES), sds(B, T, LANES)],
        compiler_params=_cparams(("arbitrary", "arbitrary")),
    )(xa, mod, win, wqb, wkvb, gqa, gkva, hs, *tabs_m, *tabs_g)


def _softmax_pv(q, k, v):
    s = lax.dot_general(q, k, (((1,), (1,)), ((), ())), preferred_element_type=_f32)
    m = jnp.max(s, axis=-1, keepdims=True)
    e = jnp.exp(s - m)
    l = jnp.sum(e, axis=-1, keepdims=True)
    pv = jnp.dot(e.astype(_bf16), v, preferred_element_type=_f32)
    return pv * (1.0 / l)


def _attn_kernel(qm_ref, km_ref, vm_ref, qg_ref, kg_ref, vg_ref, o_ref, *, n_ctx, n_ctx_tiles):
    tq = qm_ref.shape[2]
    n_all = km_ref.shape[2]
    lane = lax.broadcasted_iota(jnp.int32, (tq, LANES), 1)
    low = lane < (LANES // 2)

    def run(klen):
        def mla_pair(p, carry):
            v = vm_ref[0, p, 0:klen, :]
            o0 = _softmax_pv(qm_ref[0, 2 * p], km_ref[0, 2 * p, 0:klen, :], v)
            o1 = _softmax_pv(qm_ref[0, 2 * p + 1], km_ref[0, 2 * p + 1, 0:klen, :], v)
            o_ref[0, p] = jnp.where(low, o0, o1).astype(_bf16)
            return carry

        lax.fori_loop(0, MLA_HEADS // 2, mla_pair, 0)

        def gqa_slab(j, carry):
            qs = qg_ref[0, j]
            k = kg_ref[0, 0:klen, :]
            v = vg_ref[0, 0:klen, :]
            o0 = _softmax_pv(jnp.where(low, qs, jnp.zeros_like(qs)), k, v)
            o1 = _softmax_pv(jnp.where(low, jnp.zeros_like(qs), qs), k, v)
            o_ref[0, MLA_HEADS // 2 + j] = jnp.where(low, o0, o1).astype(_bf16)
            return carry

        lax.fori_loop(0, GQA_HEADS // 2, gqa_slab, 0)

    i = pl.program_id(1)

    @pl.when(i < n_ctx_tiles)
    def _():
        run(n_ctx)

    @pl.when(i >= n_ctx_tiles)
    def _():
        run(n_all)


def _attention(qm, km, vm, qg, kg, vg, tq, n_ctx):
    B, _, T, _ = qm.shape
    qspec = lambda n: pl.BlockSpec((1, n, tq, LANES), lambda b, i: (b, 0, i, 0))
    kspec = lambda n: pl.BlockSpec((1, n, T, LANES), lambda b, i: (b, 0, 0, 0))
    k3 = pl.BlockSpec((1, T, LANES), lambda b, i: (b, 0, 0))
    return pl.pallas_call(
        functools.partial(_attn_kernel, n_ctx=n_ctx, n_ctx_tiles=n_ctx // tq),
        grid=(B, T // tq),
        in_specs=[qspec(8), kspec(8), kspec(4), qspec(4), k3, k3],
        out_specs=qspec(8),
        out_shape=jax.ShapeDtypeStruct((B, 8, T, LANES), _bf16),
        compiler_params=_cparams(("arbitrary", "arbitrary")),
    )(qm, km, vm, qg, kg, vg)


def _wo_ln_kernel(x_ref, o_ref, mod_ref, wo_ref, g_ref, b_ref, out_ref, *, alpha):
    o = jnp.concatenate([o_ref[0, p] for p in range(o_ref.shape[1])], axis=-1)
    mix = jnp.dot(o, wo_ref[...], preferred_element_type=_f32)
    out_ref[0] = _layer_norm(alpha * x_ref[0] + mod_ref[2:3, :] * mix, g_ref[...], b_ref[...])


def _wo_ln(xa, o, mod, wo, g, b, tm, n_ctx_tiles, alpha):
    B, T, D = xa.shape
    full = lambda a: pl.BlockSpec(a.shape, lambda bb, i: (0,) * a.ndim)
    return pl.pallas_call(
        functools.partial(_wo_ln_kernel, alpha=alpha),
        grid=(B, T // tm),
        in_specs=[pl.BlockSpec((1, tm, D), lambda bb, i: (bb, i, 0)),
                  pl.BlockSpec((1, 8, tm, LANES), lambda bb, i: (bb, 0, i, 0)),
                  pl.BlockSpec((None, None, 6, D), lambda bb, i: (bb, jnp.where(i >= n_ctx_tiles, 1, 0), 0, 0)),
                  full(wo), full(g), full(b)],
        out_specs=pl.BlockSpec((1, tm, D), lambda bb, i: (bb, i, 0)),
        out_shape=jax.ShapeDtypeStruct((B, T, D), _f32),
        compiler_params=_cparams(("arbitrary", "arbitrary")),
    )(xa, o, mod, wo, g, b)


def _router_kernel(x_ref, mod_ref, wr_ref, br_ref, h_ref, info_ref, cnt_ref, carry_ref):
    tm = x_ref.shape[1]
    first = (pl.program_id(0) == 0) & (pl.program_id(1) == 0)

    @pl.when(first)
    def _():
        carry_ref[...] = jnp.zeros_like(carry_ref)

    h = x_ref[0] * (1.0 + mod_ref[4:5, :]) + mod_ref[3:4, :]
    h_ref[0] = h
    logits = jnp.dot(h.astype(_bf16), wr_ref[...], preferred_element_type=_f32) + br_ref[...]
    lane_i = lax.broadcasted_iota(jnp.int32, (tm, LANES), 1)
    lane = lane_i.astype(_f32)
    neg = jnp.float32(-jnp.inf)
    big = jnp.float32(LANES)

    def arg_first_max(vals):
        mx = jnp.max(vals, axis=-1, keepdims=True)
        idx = jnp.min(jnp.where(vals == mx, lane, big), axis=-1, keepdims=True)
        return mx, idx

    is_grp = (lane_i >= N_EXPERTS) & (lane_i < N_EXPERTS + N_GROUPS)
    lg = jnp.where(is_grp, logits, neg)
    gmax, gidx = arg_first_max(lg)
    p_star = 1.0 / jnp.sum(jnp.exp(lg - gmax), axis=-1, keepdims=True)
    g_star = gidx - N_EXPERTS
    in_grp = (lane_i < N_EXPERTS) & ((lane_i >> 3).astype(_f32) == g_star)
    le = jnp.where(in_grp, logits, neg)
    v1, e1 = arg_first_max(le)
    v2, e2 = arg_first_max(jnp.where(lane == e1, neg, le))
    t = jnp.exp(v2 - v1)
    w1 = p_star / (1.0 + t)
    w2 = p_star * t / (1.0 + t)

    onehot = ((lane == e1) | (lane == e2)).astype(_f32)
    r = lax.broadcasted_iota(jnp.int32, (tm, tm), 0)
    c = lax.broadcasted_iota(jnp.int32, (tm, tm), 1)
    tri = (c < r).astype(_bf16)
    prefix = jnp.dot(tri, onehot.astype(_bf16), preferred_element_type=_f32) + carry_ref[...]
    r1 = jnp.sum(jnp.where(lane == e1, prefix, 0.0), axis=-1, keepdims=True)
    r2 = jnp.sum(jnp.where(lane == e2, prefix, 0.0), axis=-1, keepdims=True)
    carry_ref[...] += jnp.sum(onehot, axis=0, keepdims=True)
    cnt_ref[...] = carry_ref[...]

    l8 = lax.broadcasted_iota(jnp.int32, (tm, 8), 1)
    cols = [e1, e2, w1, w2, r1, r2]
    info = jnp.zeros((tm, 8), _f32)
    for k, col in enumerate(cols):
        info = jnp.where(l8 == k, col, info)
    info_ref[0] = info


def _router(x1, mod, wr, br, tm, n_ctx_tiles):
    B, T, D = x1.shape
    full = lambda a: pl.BlockSpec(a.shape, lambda bb, i: (0,) * a.ndim)
    return pl.pallas_call(
        _router_kernel,
        grid=(B, T // tm),
        in_specs=[pl.BlockSpec((1, tm, D), lambda bb, i: (bb, i, 0)),
                  pl.BlockSpec((None, None, 6, D), lambda bb, i: (bb, jnp.where(i >= n_ctx_tiles, 1, 0), 0, 0)),
                  full(wr), full(br)],
        out_specs=[pl.BlockSpec((1, tm, D), lambda bb, i: (bb, i, 0)),
                   pl.BlockSpec((1, tm, 8), lambda bb, i: (bb, i, 0)),
                   pl.BlockSpec((1, LANES), lambda bb, i: (0, 0))],
        out_shape=[jax.ShapeDtypeStruct((B, T, D), _f32),
                   jax.ShapeDtypeStruct((B, T, 8), _f32),
                   jax.ShapeDtypeStruct((1, LANES), _f32)],
        scratch_shapes=[pltpu.VMEM((1, LANES), _f32)],
        compiler_params=_cparams(("arbitrary", "arbitrary")),
    )(x1, mod, wr, br)


def _row_copy(src_ref, row, dst_ref, slot, sem):
    return pltpu.make_async_copy(src_ref.at[pl.ds(row, 1), :], dst_ref.at[pl.ds(slot, 1), :], sem)


def _issue_rows(src_ref, idx_ref, dst_ref, n, sem):
    for r in range(n):
        _row_copy(src_ref, idx_ref[0, 0, r], dst_ref, r, sem).start()


def _wait_rows(src_ref, dst_ref, n, sem):
    for r in range(n):
        _row_copy(src_ref, 0, dst_ref, r, sem).wait()


def _expert_kernel(be_ref, nu_ref, tok_ref, tokn_ref, h_ref, wg_ref, wu_ref, wd_ref, y_ref,
                   xbuf0, xbuf1, wg_bf, wu_bf, wd_bf, sem):
    i = pl.program_id(0)
    rows = xbuf0.shape[0]
    n_used = nu_ref[0]
    active = i < n_used
    even = (i % 2) == 0

    @pl.when(i == 0)
    def _():
        _issue_rows(h_ref, tok_ref, xbuf0, rows, sem.at[0])

    new_expert = active & ((i == 0) | (be_ref[i] != be_ref[jnp.maximum(i - 1, 0)]))

    @pl.when(new_expert)
    def _():
        wg_bf[...] = wg_ref[0].astype(_bf16)
        wu_bf[...] = wu_ref[0].astype(_bf16)
        wd_bf[...] = wd_ref[0].astype(_bf16)

    def step(cur, nxt, s_cur, s_nxt):
        _wait_rows(h_ref, cur, rows, sem.at[s_cur])
        xb = cur[...].astype(_bf16)
        _issue_rows(h_ref, tokn_ref, nxt, rows, sem.at[s_nxt])
        g = jnp.dot(xb, wg_bf[...], preferred_element_type=_f32)
        u = jnp.dot(xb, wu_bf[...], preferred_element_type=_f32)
        a = (g * _sigmoid(g) * u).astype(_bf16)
        y_ref[...] = jnp.dot(a, wd_bf[...], preferred_element_type=_f32)

    @pl.when(active & even)
    def _():
        step(xbuf0, xbuf1, 0, 1)

    @pl.when(active & jnp.logical_not(even))
    def _():
        step(xbuf1, xbuf0, 1, 0)

    @pl.when((i == n_used) & even)
    def _():
        _wait_rows(h_ref, xbuf0, rows, sem.at[0])

    @pl.when((i == n_used) & jnp.logical_not(even))
    def _():
        _wait_rows(h_ref, xbuf1, rows, sem.at[1])

    @pl.when(jnp.logical_not(active))
    def _():
        y_ref[...] = jnp.zeros_like(y_ref)


def _experts(h2d, slot_tok, block_e, n_used, w_gate, w_up, w_down, rows):
    n_blocks = slot_tok.shape[0]
    _, D, F = w_gate.shape
    grid_spec = pltpu.PrefetchScalarGridSpec(
        num_scalar_prefetch=2,
        grid=(n_blocks,),
        in_specs=[pl.BlockSpec((1, 1, rows), lambda i, be, nu: (i, 0, 0), memory_space=pltpu.SMEM),
                  pl.BlockSpec((1, 1, rows), lambda i, be, nu: (jnp.minimum(i + 1, n_blocks - 1), 0, 0),
                               memory_space=pltpu.SMEM),
                  pl.BlockSpec(memory_space=pl.ANY),
                  pl.BlockSpec((1, D, F), lambda i, be, nu: (be[i], 0, 0)),
                  pl.BlockSpec((1, D, F), lambda i, be, nu: (be[i], 0, 0)),
                  pl.BlockSpec((1, F, D), lambda i, be, nu: (be[i], 0, 0))],
        out_specs=pl.BlockSpec((rows, D), lambda i, be, nu: (i, 0)),
        scratch_shapes=[pltpu.VMEM((rows, D), _f32), pltpu.VMEM((rows, D), _f32),
                        pltpu.VMEM((D, F), _bf16), pltpu.VMEM((D, F), _bf16), pltpu.VMEM((F, D), _bf16),
                        pltpu.SemaphoreType.DMA((2,))],
    )
    return pl.pallas_call(
        _expert_kernel,
        grid_spec=grid_spec,
        out_shape=jax.ShapeDtypeStruct((n_blocks * rows, D), _f32),
        compiler_params=_cparams(("arbitrary",)),
    )(block_e, n_used, slot_tok, slot_tok, h2d, w_gate, w_up, w_down)


def _combine_kernel(dest_ref, destn_ref, x_ref, mod_ref, info_ref, yb_ref, g_ref, b_ref, out_ref,
                    ybuf0, ybuf1, sem, *, alpha, n_steps):
    tm = x_ref.shape[0]
    g_id = pl.program_id(0)
    even = (g_id % 2) == 0
    more = g_id + 1 < n_steps

    @pl.when(g_id == 0)
    def _():
        _issue_rows(yb_ref, dest_ref, ybuf0, 2 * tm, sem.at[0])

    @pl.when(more & even)
    def _():
        _issue_rows(yb_ref, destn_ref, ybuf1, 2 * tm, sem.at[1])

    @pl.when(more & jnp.logical_not(even))
    def _():
        _issue_rows(yb_ref, destn_ref, ybuf0, 2 * tm, sem.at[0])

    def finish(buf, s):
        _wait_rows(yb_ref, buf, 2 * tm, sem.at[s])
        info = info_ref[...]
        y = info[:, 2:3] * buf[0:tm, :] + info[:, 3:4] * buf[tm:2 * tm, :]
        out_ref[...] = _layer_norm(alpha * x_ref[...] + mod_ref[5:6, :] * y, g_ref[...], b_ref[...])

    @pl.when(even)
    def _():
        finish(ybuf0, 0)

    @pl.when(jnp.logical_not(even))
    def _():
        finish(ybuf1, 1)


def _combine_ln(x1, mod, info, dest, yb, g, b, tm, n_ctx_tiles, alpha, skip_ctx):
    B, T, D = x1.shape
    nt = T // tm
    off = n_ctx_tiles if skip_ctx else 0
    ntl = nt - off
    n_steps = B * ntl
    tile = lambda s: (s // ntl) * nt + off + s % ntl
    full = lambda a: pl.BlockSpec(a.shape, lambda s: (0,) * a.ndim)
    out = pl.pallas_call(
        functools.partial(_combine_kernel, alpha=alpha, n_steps=n_steps),
        grid=(n_steps,),
        in_specs=[pl.BlockSpec((1, 1, 2 * tm), lambda s: (tile(s), 0, 0), memory_space=pltpu.SMEM),
                  pl.BlockSpec((1, 1, 2 * tm), lambda s: (tile(jnp.minimum(s + 1, n_steps - 1)), 0, 0),
                               memory_space=pltpu.SMEM),
                  pl.BlockSpec((tm, D), lambda s: (tile(s), 0)),
                  pl.BlockSpec((None, None, 6, D),
                               lambda s: (s // ntl, jnp.where(s % ntl + off >= n_ctx_tiles, 1, 0), 0, 0)),
                  pl.BlockSpec((tm, 8), lambda s: (tile(s), 0)),
                  pl.BlockSpec(memory_space=pl.ANY),
                  full(g), full(b)],
        out_specs=pl.BlockSpec((tm, D), lambda s: (s, 0)),
        out_shape=jax.ShapeDtypeStruct((n_steps * tm, D), _f32),
        scratch_shapes=[pltpu.VMEM((2 * tm, D), _f32), pltpu.VMEM((2 * tm, D), _f32),
                        pltpu.SemaphoreType.DMA((2,))],
        compiler_params=_cparams(("arbitrary",)),
    )(dest, dest, x1.reshape(B * T, D), mod, info.reshape(B * T, 8), yb, g, b)
    return out.reshape(B, ntl * tm, D)


def _prep_weights(w_in, w_q_b, w_kv_b, w_o, w_router_grp, b_router_grp, w_router_exp, b_router_exp):
    L, D, _ = w_in.shape
    zeros = lambda *s: jnp.zeros((L,) + s, _f32)
    kr = jnp.concatenate([zeros(D, 64), w_in[:, :, 640:672], zeros(D, 32)], axis=-1)
    gq = w_in[:, :, 672:1184].reshape(L, D, GQA_HEADS, HEAD_DIM)
    gq = jnp.stack([gq[:, :, :4], gq[:, :, 4:]], axis=3).reshape(L, D, 512)
    win = jnp.concatenate([w_in[:, :, 0:640], kr, gq, w_in[:, :, 1184:1440]], axis=-1).astype(_bf16)

    qb = w_q_b.reshape(L, MLA_Q_LORA, MLA_HEADS, MLA_QK)
    wqb = jnp.concatenate([qb, zeros(MLA_Q_LORA, MLA_HEADS, LANES - MLA_QK)], axis=-1)
    wqb = wqb.reshape(L, MLA_Q_LORA, MLA_HEADS * LANES).astype(_bf16)

    kvb = w_kv_b.reshape(L, MLA_KV_LORA, MLA_HEADS, MLA_NOPE + MLA_V)
    kpart = jnp.concatenate([kvb[..., :MLA_NOPE], zeros(MLA_KV_LORA, MLA_HEADS, LANES - MLA_NOPE)], axis=-1)
    vpart = kvb[..., MLA_NOPE:].reshape(L, MLA_KV_LORA, MLA_HEADS * MLA_V)
    wkvb = jnp.concatenate([kpart.reshape(L, MLA_KV_LORA, MLA_HEADS * LANES), vpart], axis=-1).astype(_bf16)

    n_mla = MLA_HEADS * MLA_V
    og = w_o[:, n_mla:, :].reshape(L, GQA_HEADS, HEAD_DIM, D)
    og = jnp.stack([og[:, :4], og[:, 4:]], axis=2).reshape(L, GQA_HEADS * HEAD_DIM, D)
    wo = jnp.concatenate([w_o[:, :n_mla, :], og], axis=1).astype(_bf16)

    pad = LANES - N_EXPERTS - N_GROUPS
    wr = jnp.concatenate([w_router_exp, w_router_grp, zeros(D, pad)], axis=-1).astype(_bf16)
    br = jnp.concatenate([b_router_exp, b_router_grp, jnp.zeros((L, pad), _f32)], axis=-1).reshape(L, 1, LANES)
    return win, wqb, wkvb, wo, wr, br


def _rope_tables(S, C, g_gqa_q, g_gqa_k):
    pos = jnp.arange(S)
    row = (pos // GRID_W).astype(_f32)
    col = (pos % GRID_W).astype(_f32)

    def angles(half):
        inv = ROPE_BASE ** (-(jnp.arange(0, half, 2, dtype=_f32) / half))
        ar, ac = row[:, None] * inv, col[:, None] * inv
        return jnp.concatenate([ar, ar, ac, ac], axis=-1)

    def cos_sin(half):
        ang = angles(half)
        sign = jnp.tile(jnp.concatenate([-jnp.ones(half // 2), jnp.ones(half // 2)]), 2).astype(_f32)
        cos = jnp.concatenate([jnp.ones((C, 2 * half), _f32), jnp.cos(ang)], axis=0)
        sin = jnp.concatenate([jnp.zeros((C, 2 * half), _f32), jnp.sin(ang) * sign], axis=0)
        return cos, sin

    T = S + C
    cm, sm = cos_sin(MLA_ROPE // 2)
    a_m = jnp.concatenate([jnp.ones((T, 64), _f32), cm, jnp.ones((T, 32), _f32)], axis=-1)
    b_m = jnp.concatenate([jnp.zeros((T, 64), _f32), sm, jnp.zeros((T, 32), _f32)], axis=-1)
    tabs_m = (a_m * MLA_SCALE, b_m * MLA_SCALE, a_m, b_m)

    cg, sg = cos_sin(HEAD_DIM // 2)
    swap = jnp.arange(HEAD_DIM) ^ (HEAD_DIM // 4)

    def gqa(g, scale):
        a = g[:, None, :] * cg[None] * scale
        b = g[:, swap][:, None, :] * sg[None] * scale
        return jnp.tile(a, (1, 1, 2)), jnp.tile(b, (1, 1, 2))

    aq, bq = gqa(g_gqa_q, GQA_SCALE)
    ak, bk = gqa(g_gqa_k, 1.0)
    return tabs_m, (aq, bq, ak, bk)


def kernel(x, c, ctx, c_ctx, w_ada, b_ada, w_in, g_q_a, w_q_b, g_kv_a, w_kv_b, g_gqa_q, g_gqa_k,
           w_o, ln1_g, ln1_b, w_router_grp, b_router_grp, w_router_exp, b_router_exp,
           w_gate, w_up, w_down, ln2_g, ln2_b):
    B, S, D = x.shape
    C = ctx.shape[1]
    L = w_in.shape[0]
    T = C + S
    alpha = (2.0 * L) ** 0.25
    tm = 256 if C % 256 == 0 else 128
    assert C % tm == 0 and S % tm == 0
    n_ctx_tiles = C // tm
    rows = MOE_ROWS

    R = -(-(B + 1) // 8) * 8
    cc = jnp.concatenate([c, c_ctx[None, :], jnp.zeros((R - B - 1, D), _f32)], axis=0)
    mods = _adaln(cc, w_ada, b_ada)
    lat = mods[:, :B].reshape(L, B, 1, 6, D)
    cx = jnp.broadcast_to(mods[:, B].reshape(L, 1, 1, 6, D), (L, B, 1, 6, D))
    mods = jnp.concatenate([cx, lat], axis=2)

    win, wqb, wkvb, wo, wr, br = _prep_weights(w_in, w_q_b, w_kv_b, w_o, w_router_grp, b_router_grp,
                                               w_router_exp, b_router_exp)
    tabs_m, tabs_g = _rope_tables(S, C, g_gqa_q, g_gqa_k)
    blk = jnp.arange(GQA_HEADS * HEAD_DIM) // HEAD_DIM
    hs = (blk[:, None] == blk[None, :]).astype(_bf16)

    n_tok = B * T
    n_assign = n_tok * TOP_K
    n_blocks = -(-n_assign // rows) + N_EXPERTS + 1
    tok_ids = jnp.repeat(jnp.arange(n_tok, dtype=jnp.int32), TOP_K)
    blk_start = jnp.arange(n_blocks, dtype=jnp.int32) * rows

    xa = jnp.concatenate([ctx, x], axis=1)
    for l in range(L):
        mod = mods[l]
        qm, km, vm, qg, kg, vg = _inproj(
            xa, mod, win[l], wqb[l], wkvb[l], g_q_a[l][None, :], g_kv_a[l][None, :], hs,
            tabs_m, tuple(t[l] for t in tabs_g), tm, n_ctx_tiles)
        o = _attention(qm, km, vm, qg, kg, vg, tm, C)
        x1 = _wo_ln(xa, o, mod, wo[l], ln1_g[l][None, :], ln1_b[l][None, :], tm, n_ctx_tiles, alpha)

        h, info, counts = _router(x1, mod, wr[l], br[l], tm, n_ctx_tiles)
        cnt = counts[0, :N_EXPERTS].astype(jnp.int32)
        padded = (cnt + rows - 1) // rows * rows
        pad_end = jnp.cumsum(padded)
        pad_start = pad_end - padded
        eid = info[..., 0:2].astype(jnp.int32).reshape(n_tok, TOP_K)
        rank = info[..., 4:6].astype(jnp.int32).reshape(n_tok, TOP_K)
        dest = pad_start[eid] + rank
        slot_tok = jnp.zeros((n_blocks * rows,), jnp.int32).at[dest.reshape(-1)].set(tok_ids)
        block_e = jnp.sum((blk_start[:, None] >= pad_end[None, :]).astype(jnp.int32), axis=1)
        block_e = jnp.minimum(block_e, N_EXPERTS - 1)
        n_used = (pad_end[-1:] // rows).astype(jnp.int32)

        yb = _experts(h.reshape(n_tok, D), slot_tok.reshape(n_blocks, 1, rows), block_e, n_used,
                      w_gate[l], w_up[l], w_down[l], rows)
        dest_t = dest.reshape(n_tok // tm, tm, TOP_K).transpose(0, 2, 1).reshape(n_tok // tm, 1, TOP_K * tm)
        xa = _combine_ln(x1, mod, info, dest_t, yb, ln2_g[l][None, :], ln2_b[l][None, :],
                         tm, n_ctx_tiles, alpha, skip_ctx=(l == L - 1))
    return xa
```

```python
import functools

import jax
import jax.numpy as jnp
from jax import lax
from jax.experimental import pallas as pl
from jax.experimental.pallas import tpu as pltpu

GRID_W = 64
ROPE_BASE = 10000.0
RMS_EPS = 1e-6
LN_EPS = 1e-5

MLA_HEADS = 8
MLA_Q_LORA = 384
MLA_KV_LORA = 256
MLA_NOPE = 64
MLA_ROPE = 32
MLA_V = 64
MLA_QK = MLA_NOPE + MLA_ROPE
MLA_SCALE = MLA_QK ** -0.5

HEAD_DIM = 64
GQA_HEADS = 8
GQA_KV_HEADS = 2
GQA_SCALE = HEAD_DIM ** -0.5
LOG2E = 1.4426950408889634

N_GROUPS = 4
EXPERTS_PER_GROUP = 8
N_EXPERTS = N_GROUPS * EXPERTS_PER_GROUP
TOP_K = 2

LANES = 128
MOE_ROWS = 256
VMEM_LIMIT = 56 * 1024 * 1024

_QA0, _KVA0, _KR0, _GQ0, _GK0, _GV0, _INW = 0, 384, 640, 768, 1280, 1408, 1536

_f32 = jnp.float32
_bf16 = jnp.bfloat16


def _cparams(sem):
    return pltpu.CompilerParams(dimension_semantics=sem, vmem_limit_bytes=VMEM_LIMIT)


def _sigmoid(v):
    return 1.0 / (1.0 + jnp.exp(-v))


def _layer_norm(v, g, b):
    mu = jnp.mean(v, axis=-1, keepdims=True)
    d = v - mu
    var = jnp.mean(d * d, axis=-1, keepdims=True)
    return d * lax.rsqrt(var + LN_EPS) * g + b


def _rot_half(y, dist):
    lane = lax.broadcasted_iota(jnp.int32, y.shape, y.ndim - 1)
    n = y.shape[-1]
    fwd = pltpu.roll(y, n - dist, y.ndim - 1)
    bwd = pltpu.roll(y, dist, y.ndim - 1)
    return jnp.where((lane & dist) == 0, fwd, bwd)


def _adaln_kernel(c_ref, w_ref, b_ref, o_ref):
    cc = c_ref[...]
    s = (cc * _sigmoid(cc)).astype(_bf16)
    o_ref[0] = jnp.dot(s, w_ref[0].astype(_bf16), preferred_element_type=_f32) + b_ref[0]


def _adaln(cc, w_ada, b_ada):
    L, D, N = w_ada.shape
    R = cc.shape[0]
    tn = 1536 if N % 1536 == 0 else N
    return pl.pallas_call(
        _adaln_kernel,
        grid=(L, N // tn),
        in_specs=[pl.BlockSpec((R, D), lambda l, j: (0, 0)),
                  pl.BlockSpec((1, D, tn), lambda l, j: (l, 0, j)),
                  pl.BlockSpec((1, 1, tn), lambda l, j: (l, 0, j))],
        out_specs=pl.BlockSpec((1, R, tn), lambda l, j: (l, 0, j)),
        out_shape=jax.ShapeDtypeStruct((L, R, N), _f32),
        compiler_params=_cparams(("arbitrary", "arbitrary")),
    )(cc, w_ada, b_ada.reshape(L, 1, N))


def _inproj_kernel(x_ref, mod_ref, win_ref, wqb_ref, wkvb_ref, gqa_ref, gkva_ref, hs_ref,
                   aqm_ref, bqm_ref, akm_ref, bkm_ref, aqg_ref, bqg_ref, akg_ref, bkg_ref,
                   qm_ref, km_ref, vm_ref, qg_ref, kg_ref, vg_ref):
    x = x_ref[0]
    h = (x * (1.0 + mod_ref[1:2, :]) + mod_ref[0:1, :]).astype(_bf16)
    z = jnp.dot(h, win_ref[...], preferred_element_type=_f32)

    q_a = z[:, _QA0:_KVA0]
    qn = q_a * lax.rsqrt(jnp.mean(q_a * q_a, axis=-1, keepdims=True) + RMS_EPS) * gqa_ref[...]
    q = jnp.dot(qn.astype(_bf16), wqb_ref[...], preferred_element_type=_f32)
    kv_a = z[:, _KVA0:_KR0]
    kvn = kv_a * lax.rsqrt(jnp.mean(kv_a * kv_a, axis=-1, keepdims=True) + RMS_EPS) * gkva_ref[...]
    kv = jnp.dot(kvn.astype(_bf16), wkvb_ref[...], preferred_element_type=_f32)
    aq, bq = aqm_ref[...], bqm_ref[...]
    for hd in range(MLA_HEADS):
        qh = q[:, hd * LANES:(hd + 1) * LANES]
        qm_ref[0, hd] = (qh * aq + _rot_half(qh, 8) * bq).astype(_bf16)
    kr = z[:, _KR0:_GQ0]
    kr = kr * akm_ref[...] + _rot_half(kr, 8) * bkm_ref[...]
    for hd in range(MLA_HEADS):
        km_ref[0, hd] = (kv[:, hd * LANES:(hd + 1) * LANES] + kr).astype(_bf16)
    v0 = MLA_HEADS * LANES
    for p in range(MLA_HEADS // 2):
        vm_ref[0, p] = kv[:, v0 + p * LANES:v0 + (p + 1) * LANES].astype(_bf16)

    def head_rs(t, hs):
        sq = t * t
        hi = sq.astype(_bf16)
        lo = (sq - hi.astype(_f32)).astype(_bf16)
        ssum = (jnp.dot(hi, hs, preferred_element_type=_f32)
                + jnp.dot(lo, hs, preferred_element_type=_f32))
        return lax.rsqrt(ssum * (1.0 / HEAD_DIM) + RMS_EPS)

    gq = z[:, _GQ0:_GK0]
    yq = gq * head_rs(gq, hs_ref[...])
    aqg, bqg = aqg_ref[...], bqg_ref[...]
    for j in range(GQA_HEADS // 2):
        yj = yq[:, j * LANES:(j + 1) * LANES]
        qg_ref[0, j] = (yj * aqg + _rot_half(yj, 16) * bqg).astype(_bf16)
    gk = z[:, _GK0:_GV0]
    yk = gk * head_rs(gk, hs_ref[0:LANES, 0:LANES])
    kg_ref[0] = (yk * akg_ref[...] + _rot_half(yk, 16) * bkg_ref[...]).astype(_bf16)
    vg_ref[0] = z[:, _GV0:_INW].astype(_bf16)


def _inproj(xa, mod, win, wqb, wkvb, gqa, gkva, hs, tabs_m, tabs_g, tm, n_ctx_tiles):
    B, T, D = xa.shape
    nt = T // tm
    full = lambda a: pl.BlockSpec(a.shape, lambda b, i: (0,) * a.ndim)
    tab = pl.BlockSpec((tm, LANES), lambda b, i: (i, 0))
    out4 = lambda n: pl.BlockSpec((1, n, tm, LANES), lambda b, i: (b, 0, i, 0))
    out3 = pl.BlockSpec((1, tm, LANES), lambda b, i: (b, i, 0))
    sds = lambda *s: jax.ShapeDtypeStruct(s, _bf16)
    return pl.pallas_call(
        _inproj_kernel,
        grid=(B, nt),
        in_specs=[pl.BlockSpec((1, tm, D), lambda b, i: (b, i, 0)),
                  pl.BlockSpec((None, None, 6, D), lambda b, i: (b, jnp.where(i >= n_ctx_tiles, 1, 0), 0, 0)),
                  full(win), full(wqb), full(wkvb), full(gqa), full(gkva), full(hs),
                  tab, tab, tab, tab, tab, tab, tab, tab],
        out_specs=[out4(8), out4(8), out4(4), out4(4), out3, out3],
        out_shape=[sds(B, 8, T, LANES), sds(B, 8, T, LANES), sds(B, 4, T, LANES),
                   sds(B, 4, T, LANES), sds(B, T, LANES), sds(B, T, LANES)],
        compiler_params=_cparams(("arbitrary", "arbitrary")),
    )(xa, mod, win, wqb, wkvb, gqa, gkva, hs, *tabs_m, *tabs_g)


def _softmax_pv(q, k, v):
    s = lax.dot_general(q, k, (((1,), (1,)), ((), ())), preferred_element_type=_f32)
    m = jnp.max(s, axis=-1, keepdims=True)
    e = jnp.exp2(s - m)
    l = jnp.sum(e, axis=-1, keepdims=True)
    pv = jnp.dot(e.astype(_bf16), v, preferred_element_type=_f32)
    return pv * (1.0 / l)


def _attn_kernel(qm_ref, km_ref, vm_ref, qg_ref, kg_ref, vg_ref, o_ref, *, n_ctx, n_ctx_tiles):
    tq = qm_ref.shape[2]
    n_all = km_ref.shape[2]
    lane = lax.broadcasted_iota(jnp.int32, (tq, LANES), 1)
    low = lane < (LANES // 2)

    def run(klen):
        def mla_pair(p, carry):
            v = vm_ref[0, p, 0:klen, :]
            o0 = _softmax_pv(qm_ref[0, 2 * p], km_ref[0, 2 * p, 0:klen, :], v)
            o1 = _softmax_pv(qm_ref[0, 2 * p + 1], km_ref[0, 2 * p + 1, 0:klen, :], v)
            o_ref[0, p] = jnp.where(low, o0, o1).astype(_bf16)
            return carry

        lax.fori_loop(0, MLA_HEADS // 2, mla_pair, 0, unroll=True)

        def gqa_slab(j, carry):
            qs = qg_ref[0, j]
            k = kg_ref[0, 0:klen, :]
            v = vg_ref[0, 0:klen, :]
            o0 = _softmax_pv(jnp.where(low, qs, jnp.zeros_like(qs)), k, v)
            o1 = _softmax_pv(jnp.where(low, jnp.zeros_like(qs), qs), k, v)
            o_ref[0, MLA_HEADS // 2 + j] = jnp.where(low, o0, o1).astype(_bf16)
            return carry

        lax.fori_loop(0, GQA_HEADS // 2, gqa_slab, 0, unroll=True)

    i = pl.program_id(1)

    @pl.when(i < n_ctx_tiles)
    def _():
        run(n_ctx)

    @pl.when(i >= n_ctx_tiles)
    def _():
        run(n_all)


def _attention(qm, km, vm, qg, kg, vg, tq, n_ctx):
    B, _, T, _ = qm.shape
    qspec = lambda n: pl.BlockSpec((1, n, tq, LANES), lambda b, i: (b, 0, i, 0))
    kspec = lambda n: pl.BlockSpec((1, n, T, LANES), lambda b, i: (b, 0, 0, 0))
    k3 = pl.BlockSpec((1, T, LANES), lambda b, i: (b, 0, 0))
    return pl.pallas_call(
        functools.partial(_attn_kernel, n_ctx=n_ctx, n_ctx_tiles=n_ctx // tq),
        grid=(B, T // tq),
        in_specs=[qspec(8), kspec(8), kspec(4), qspec(4), k3, k3],
        out_specs=qspec(8),
        out_shape=jax.ShapeDtypeStruct((B, 8, T, LANES), _bf16),
        compiler_params=_cparams(("arbitrary", "arbitrary")),
    )(qm, km, vm, qg, kg, vg)


def _wo_ln_kernel(x_ref, o_ref, mod_ref, wo_ref, g_ref, b_ref, out_ref, *, alpha):
    o = jnp.concatenate([o_ref[0, p] for p in range(o_ref.shape[1])], axis=-1)
    mix = jnp.dot(o, wo_ref[...], preferred_element_type=_f32)
    out_ref[0] = _layer_norm(alpha * x_ref[0] + mod_ref[2:3, :] * mix, g_ref[...], b_ref[...])


def _wo_ln(xa, o, mod, wo, g, b, tm, n_ctx_tiles, alpha):
    B, T, D = xa.shape
    full = lambda a: pl.BlockSpec(a.shape, lambda bb, i: (0,) * a.ndim)
    return pl.pallas_call(
        functools.partial(_wo_ln_kernel, alpha=alpha),
        grid=(B, T // tm),
        in_specs=[pl.BlockSpec((1, tm, D), lambda bb, i: (bb, i, 0)),
                  pl.BlockSpec((1, 8, tm, LANES), lambda bb, i: (bb, 0, i, 0)),
                  pl.BlockSpec((None, None, 6, D), lambda bb, i: (bb, jnp.where(i >= n_ctx_tiles, 1, 0), 0, 0)),
                  full(wo), full(g), full(b)],
        out_specs=pl.BlockSpec((1, tm, D), lambda bb, i: (bb, i, 0)),
        out_shape=jax.ShapeDtypeStruct((B, T, D), _f32),
        compiler_params=_cparams(("arbitrary", "arbitrary")),
    )(xa, o, mod, wo, g, b)


def _router_kernel(x_ref, mod_ref, wr_ref, br_ref, h_ref, info_ref, cnt_ref, carry_ref):
    tm = x_ref.shape[1]
    first = (pl.program_id(0) == 0) & (pl.program_id(1) == 0)

    @pl.when(first)
    def _():
        carry_ref[...] = jnp.zeros_like(carry_ref)

    h = x_ref[0] * (1.0 + mod_ref[4:5, :]) + mod_ref[3:4, :]
    h_ref[0] = h
    logits = jnp.dot(h.astype(_bf16), wr_ref[...], preferred_element_type=_f32) + br_ref[...]
    lane_i = lax.broadcasted_iota(jnp.int32, (tm, LANES), 1)
    lane = lane_i.astype(_f32)
    neg = jnp.float32(-jnp.inf)
    big = jnp.float32(LANES)

    def arg_first_max(vals):
        mx = jnp.max(vals, axis=-1, keepdims=True)
        idx = jnp.min(jnp.where(vals == mx, lane, big), axis=-1, keepdims=True)
        return mx, idx

    is_grp = (lane_i >= N_EXPERTS) & (lane_i < N_EXPERTS + N_GROUPS)
    lg = jnp.where(is_grp, logits, neg)
    gmax, gidx = arg_first_max(lg)
    p_star = 1.0 / jnp.sum(jnp.exp(lg - gmax), axis=-1, keepdims=True)
    g_star = gidx - N_EXPERTS
    in_grp = (lane_i < N_EXPERTS) & ((lane_i >> 3).astype(_f32) == g_star)
    le = jnp.where(in_grp, logits, neg)
    v1, e1 = arg_first_max(le)
    v2, e2 = arg_first_max(jnp.where(lane == e1, neg, le))
    t = jnp.exp(v2 - v1)
    w1 = p_star / (1.0 + t)
    w2 = p_star * t / (1.0 + t)

    onehot = ((lane == e1) | (lane == e2)).astype(_f32)
    r = lax.broadcasted_iota(jnp.int32, (tm, tm), 0)
    c = lax.broadcasted_iota(jnp.int32, (tm, tm), 1)
    tri = (c < r).astype(_bf16)
    prefix = jnp.dot(tri, onehot.astype(_bf16), preferred_element_type=_f32) + carry_ref[...]
    r1 = jnp.sum(jnp.where(lane == e1, prefix, 0.0), axis=-1, keepdims=True)
    r2 = jnp.sum(jnp.where(lane == e2, prefix, 0.0), axis=-1, keepdims=True)
    carry_ref[...] += jnp.sum(onehot, axis=0, keepdims=True)
    cnt_ref[...] = carry_ref[...]

    l8 = lax.broadcasted_iota(jnp.int32, (tm, 8), 1)
    cols = [e1, e2, w1, w2, r1, r2]
    info = jnp.zeros((tm, 8), _f32)
    for k, col in enumerate(cols):
        info = jnp.where(l8 == k, col, info)
    info_ref[0] = info


def _router(x1, mod, wr, br, tm, n_ctx_tiles):
    B, T, D = x1.shape
    full = lambda a: pl.BlockSpec(a.shape, lambda bb, i: (0,) * a.ndim)
    return pl.pallas_call(
        _router_kernel,
        grid=(B, T // tm),
        in_specs=[pl.BlockSpec((1, tm, D), lambda bb, i: (bb, i, 0)),
                  pl.BlockSpec((None, None, 6, D), lambda bb, i: (bb, jnp.where(i >= n_ctx_tiles, 1, 0), 0, 0)),
                  full(wr), full(br)],
        out_specs=[pl.BlockSpec((1, tm, D), lambda bb, i: (bb, i, 0)),
                   pl.BlockSpec((1, tm, 8), lambda bb, i: (bb, i, 0)),
                   pl.BlockSpec((1, LANES), lambda bb, i: (0, 0))],
        out_shape=[jax.ShapeDtypeStruct((B, T, D), _f32),
                   jax.ShapeDtypeStruct((B, T, 8), _f32),
                   jax.ShapeDtypeStruct((1, LANES), _f32)],
        scratch_shapes=[pltpu.VMEM((1, LANES), _f32)],
        compiler_params=_cparams(("arbitrary", "arbitrary")),
    )(x1, mod, wr, br)


def _row_copy(src_ref, row, dst_ref, slot, sem):
    return pltpu.make_async_copy(src_ref.at[pl.ds(row, 1), :], dst_ref.at[pl.ds(slot, 1), :], sem)


def _issue_rows(src_ref, idx_ref, dst_ref, n, sem):
    for r in range(n):
        _row_copy(src_ref, idx_ref[0, 0, r], dst_ref, r, sem).start()


def _wait_rows(src_ref, dst_ref, n, sem):
    for r in range(n):
        _row_copy(src_ref, 0, dst_ref, r, sem).wait()


def _expert_kernel(be_ref, nu_ref, tok0_ref, tok1_ref, tok2_ref, h_ref, wg_ref, wu_ref, wd_ref, y_ref,
                   xbuf0, xbuf1, xbuf2, wg_bf, wu_bf, wd_bf, sem):
    i = pl.program_id(0)
    rows = xbuf0.shape[0]
    n_used = nu_ref[0]
    active = i < n_used
    draining = (i == n_used) | (i == n_used + 1)
    phase = i % 3
    bufs = (xbuf0, xbuf1, xbuf2)

    @pl.when(i == 0)
    def _():
        _issue_rows(h_ref, tok0_ref, xbuf0, rows, sem.at[0])
        _issue_rows(h_ref, tok1_ref, xbuf1, rows, sem.at[1])

    new_expert = active & ((i == 0) | (be_ref[i] != be_ref[jnp.maximum(i - 1, 0)]))

    @pl.when(new_expert)
    def _():
        wg_bf[...] = wg_ref[0, 0].astype(_bf16)
        wu_bf[...] = wu_ref[0, 0].astype(_bf16)
        wd_bf[...] = wd_ref[0, 0].astype(_bf16)

    def step(p):
        q = (p + 2) % 3
        _wait_rows(h_ref, bufs[p], rows, sem.at[p])
        xb = bufs[p][...].astype(_bf16)
        _issue_rows(h_ref, tok2_ref, bufs[q], rows, sem.at[q])
        g = jnp.dot(xb, wg_bf[...], preferred_element_type=_f32)
        u = jnp.dot(xb, wu_bf[...], preferred_element_type=_f32)
        a = (g * _sigmoid(g) * u).astype(_bf16)
        y_ref[...] = jnp.dot(a, wd_bf[...], preferred_element_type=_f32)

    for p in range(3):
        pl.when(active & (phase == p))(functools.partial(step, p))
        pl.when(draining & (phase == p))(
            functools.partial(_wait_rows, h_ref, bufs[p], rows, sem.at[p]))

    @pl.when(jnp.logical_not(active))
    def _():
        y_ref[...] = jnp.zeros_like(y_ref)


def _experts(h2d, slot_tok, block_e, n_used, w_gate, w_up, w_down, layer, rows):
    n_blocks = slot_tok.shape[0]
    _, _, D, F = w_gate.shape
    tok = lambda k: pl.BlockSpec((1, 1, rows), lambda i, be, nu: (jnp.minimum(i + k, n_blocks - 1), 0, 0),
                                 memory_space=pltpu.SMEM)
    grid_spec = pltpu.PrefetchScalarGridSpec(
        num_scalar_prefetch=2,
        grid=(n_blocks,),
        in_specs=[tok(0), tok(1), tok(2),
                  pl.BlockSpec(memory_space=pl.ANY),
                  pl.BlockSpec((1, 1, D, F), lambda i, be, nu: (layer, be[i], 0, 0)),
                  pl.BlockSpec((1, 1, D, F), lambda i, be, nu: (layer, be[i], 0, 0)),
                  pl.BlockSpec((1, 1, F, D), lambda i, be, nu: (layer, be[i], 0, 0))],
        out_specs=pl.BlockSpec((rows, D), lambda i, be, nu: (i, 0)),
        scratch_shapes=[pltpu.VMEM((rows, D), _f32), pltpu.VMEM((rows, D), _f32), pltpu.VMEM((rows, D), _f32),
                        pltpu.VMEM((D, F), _bf16), pltpu.VMEM((D, F), _bf16), pltpu.VMEM((F, D), _bf16),
                        pltpu.SemaphoreType.DMA((3,))],
    )
    return pl.pallas_call(
        _expert_kernel,
        grid_spec=grid_spec,
        out_shape=jax.ShapeDtypeStruct((n_blocks * rows, D), _f32),
        compiler_params=_cparams(("arbitrary",)),
    )(block_e, n_used, slot_tok, slot_tok, slot_tok, h2d, w_gate, w_up, w_down)


def _combine_kernel(dest_ref, destn_ref, x_ref, mod_ref, info_ref, yb_ref, g_ref, b_ref, out_ref,
                    ybuf0, ybuf1, sem, *, alpha, n_steps):
    tm = x_ref.shape[0]
    g_id = pl.program_id(0)
    even = (g_id % 2) == 0
    more = g_id + 1 < n_steps

    @pl.when(g_id == 0)
    def _():
        _issue_rows(yb_ref, dest_ref, ybuf0, 2 * tm, sem.at[0])

    @pl.when(more & even)
    def _():
        _issue_rows(yb_ref, destn_ref, ybuf1, 2 * tm, sem.at[1])

    @pl.when(more & jnp.logical_not(even))
    def _():
        _issue_rows(yb_ref, destn_ref, ybuf0, 2 * tm, sem.at[0])

    def finish(buf, s):
        _wait_rows(yb_ref, buf, 2 * tm, sem.at[s])
        info = info_ref[...]
        y = info[:, 2:3] * buf[0:tm, :] + info[:, 3:4] * buf[tm:2 * tm, :]
        out_ref[...] = _layer_norm(alpha * x_ref[...] + mod_ref[5:6, :] * y, g_ref[...], b_ref[...])

    @pl.when(even)
    def _():
        finish(ybuf0, 0)

    @pl.when(jnp.logical_not(even))
    def _():
        finish(ybuf1, 1)


def _combine_ln(x1, mod, info, dest, yb, g, b, tm, n_ctx_tiles, alpha, skip_ctx):
    B, T, D = x1.shape
    nt = T // tm
    off = n_ctx_tiles if skip_ctx else 0
    ntl = nt - off
    n_steps = B * ntl
    tile = lambda s: (s // ntl) * nt + off + s % ntl
    full = lambda a: pl.BlockSpec(a.shape, lambda s: (0,) * a.ndim)
    out = pl.pallas_call(
        functools.partial(_combine_kernel, alpha=alpha, n_steps=n_steps),
        grid=(n_steps,),
        in_specs=[pl.BlockSpec((1, 1, 2 * tm), lambda s: (tile(s), 0, 0), memory_space=pltpu.SMEM),
                  pl.BlockSpec((1, 1, 2 * tm), lambda s: (tile(jnp.minimum(s + 1, n_steps - 1)), 0, 0),
                               memory_space=pltpu.SMEM),
                  pl.BlockSpec((tm, D), lambda s: (tile(s), 0)),
                  pl.BlockSpec((None, None, 6, D),
                               lambda s: (s // ntl, jnp.where(s % ntl + off >= n_ctx_tiles, 1, 0), 0, 0)),
                  pl.BlockSpec((tm, 8), lambda s: (tile(s), 0)),
                  pl.BlockSpec(memory_space=pl.ANY),
                  full(g), full(b)],
        out_specs=pl.BlockSpec((tm, D), lambda s: (s, 0)),
        out_shape=jax.ShapeDtypeStruct((n_steps * tm, D), _f32),
        scratch_shapes=[pltpu.VMEM((2 * tm, D), _f32), pltpu.VMEM((2 * tm, D), _f32),
                        pltpu.SemaphoreType.DMA((2,))],
        compiler_params=_cparams(("arbitrary",)),
    )(dest, dest, x1.reshape(B * T, D), mod, info.reshape(B * T, 8), yb, g, b)
    return out.reshape(B, ntl * tm, D)


def _prep_weights(w_in, w_q_b, w_kv_b, w_o, w_router_grp, b_router_grp, w_router_exp, b_router_exp):
    L, D, _ = w_in.shape
    zeros = lambda *s: jnp.zeros((L,) + s, _f32)
    kr = jnp.concatenate([zeros(D, 64), w_in[:, :, 640:672], zeros(D, 32)], axis=-1)
    gq = w_in[:, :, 672:1184].reshape(L, D, GQA_HEADS, HEAD_DIM)
    gq = jnp.stack([gq[:, :, :4], gq[:, :, 4:]], axis=3).reshape(L, D, 512)
    win = jnp.concatenate([w_in[:, :, 0:640], kr, gq, w_in[:, :, 1184:1440]], axis=-1).astype(_bf16)

    qb = w_q_b.reshape(L, MLA_Q_LORA, MLA_HEADS, MLA_QK)
    wqb = jnp.concatenate([qb, zeros(MLA_Q_LORA, MLA_HEADS, LANES - MLA_QK)], axis=-1)
    wqb = wqb.reshape(L, MLA_Q_LORA, MLA_HEADS * LANES).astype(_bf16)

    kvb = w_kv_b.reshape(L, MLA_KV_LORA, MLA_HEADS, MLA_NOPE + MLA_V)
    kpart = jnp.concatenate([kvb[..., :MLA_NOPE], zeros(MLA_KV_LORA, MLA_HEADS, LANES - MLA_NOPE)], axis=-1)
    vpart = kvb[..., MLA_NOPE:].reshape(L, MLA_KV_LORA, MLA_HEADS * MLA_V)
    wkvb = jnp.concatenate([kpart.reshape(L, MLA_KV_LORA, MLA_HEADS * LANES), vpart], axis=-1).astype(_bf16)

    n_mla = MLA_HEADS * MLA_V
    og = w_o[:, n_mla:, :].reshape(L, GQA_HEADS, HEAD_DIM, D)
    og = jnp.stack([og[:, :4], og[:, 4:]], axis=2).reshape(L, GQA_HEADS * HEAD_DIM, D)
    wo = jnp.concatenate([w_o[:, :n_mla, :], og], axis=1).astype(_bf16)

    pad = LANES - N_EXPERTS - N_GROUPS
    wr = jnp.concatenate([w_router_exp, w_router_grp, zeros(D, pad)], axis=-1).astype(_bf16)
    br = jnp.concatenate([b_router_exp, b_router_grp, jnp.zeros((L, pad), _f32)], axis=-1).reshape(L, 1, LANES)
    return win, wqb, wkvb, wo, wr, br


def _rope_tables(S, C, g_gqa_q, g_gqa_k):
    pos = jnp.arange(S)
    row = (pos // GRID_W).astype(_f32)
    col = (pos % GRID_W).astype(_f32)

    def angles(half):
        inv = ROPE_BASE ** (-(jnp.arange(0, half, 2, dtype=_f32) / half))
        ar, ac = row[:, None] * inv, col[:, None] * inv
        return jnp.concatenate([ar, ar, ac, ac], axis=-1)

    def cos_sin(half):
        ang = angles(half)
        sign = jnp.tile(jnp.concatenate([-jnp.ones(half // 2), jnp.ones(half // 2)]), 2).astype(_f32)
        cos = jnp.concatenate([jnp.ones((C, 2 * half), _f32), jnp.cos(ang)], axis=0)
        sin = jnp.concatenate([jnp.zeros((C, 2 * half), _f32), jnp.sin(ang) * sign], axis=0)
        return cos, sin

    T = S + C
    cm, sm = cos_sin(MLA_ROPE // 2)
    a_m = jnp.concatenate([jnp.ones((T, 64), _f32), cm, jnp.ones((T, 32), _f32)], axis=-1)
    b_m = jnp.concatenate([jnp.zeros((T, 64), _f32), sm, jnp.zeros((T, 32), _f32)], axis=-1)
    tabs_m = (a_m * (MLA_SCALE * LOG2E), b_m * (MLA_SCALE * LOG2E), a_m, b_m)

    cg, sg = cos_sin(HEAD_DIM // 2)
    swap = jnp.arange(HEAD_DIM) ^ (HEAD_DIM // 4)

    def gqa(g, scale):
        a = g[:, None, :] * cg[None] * scale
        b = g[:, swap][:, None, :] * sg[None] * scale
        return jnp.tile(a, (1, 1, 2)), jnp.tile(b, (1, 1, 2))

    aq, bq = gqa(g_gqa_q, GQA_SCALE * LOG2E)
    ak, bk = gqa(g_gqa_k, 1.0)
    return tabs_m, (aq, bq, ak, bk)


def kernel(x, c, ctx, c_ctx, w_ada, b_ada, w_in, g_q_a, w_q_b, g_kv_a, w_kv_b, g_gqa_q, g_gqa_k,
           w_o, ln1_g, ln1_b, w_router_grp, b_router_grp, w_router_exp, b_router_exp,
           w_gate, w_up, w_down, ln2_g, ln2_b):
    B, S, D = x.shape
    C = ctx.shape[1]
    L = w_in.shape[0]
    T = C + S
    alpha = (2.0 * L) ** 0.25
    tm = 256 if C % 256 == 0 else 128
    assert C % tm == 0 and S % tm == 0
    n_ctx_tiles = C // tm
    rows = MOE_ROWS

    R = -(-(B + 1) // 8) * 8
    cc = jnp.concatenate([c, c_ctx[None, :], jnp.zeros((R - B - 1, D), _f32)], axis=0)
    mods = _adaln(cc, w_ada, b_ada)
    lat = mods[:, :B].reshape(L, B, 1, 6, D)
    cx = jnp.broadcast_to(mods[:, B].reshape(L, 1, 1, 6, D), (L, B, 1, 6, D))
    mods = jnp.concatenate([cx, lat], axis=2)

    win, wqb, wkvb, wo, wr, br = _prep_weights(w_in, w_q_b, w_kv_b, w_o, w_router_grp, b_router_grp,
                                               w_router_exp, b_router_exp)
    tabs_m, tabs_g = _rope_tables(S, C, g_gqa_q, g_gqa_k)
    blk = jnp.arange(GQA_HEADS * HEAD_DIM) // HEAD_DIM
    hs = (blk[:, None] == blk[None, :]).astype(_bf16)

    n_tok = B * T
    n_assign = n_tok * TOP_K
    n_blocks = -(-n_assign // rows) + N_EXPERTS + 2
    tok_ids = jnp.repeat(jnp.arange(n_tok, dtype=jnp.int32), TOP_K)
    blk_start = jnp.arange(n_blocks, dtype=jnp.int32) * rows

    xa = jnp.concatenate([ctx, x], axis=1)
    for l in range(L):
        mod = mods[l]
        qm, km, vm, qg, kg, vg = _inproj(
            xa, mod, win[l], wqb[l], wkvb[l], g_q_a[l][None, :], g_kv_a[l][None, :], hs,
            tabs_m, tuple(t[l] for t in tabs_g), tm, n_ctx_tiles)
        o = _attention(qm, km, vm, qg, kg, vg, tm, C)
        x1 = _wo_ln(xa, o, mod, wo[l], ln1_g[l][None, :], ln1_b[l][None, :], tm, n_ctx_tiles, alpha)

        h, info, counts = _router(x1, mod, wr[l], br[l], tm, n_ctx_tiles)
        cnt = counts[0, :N_EXPERTS].astype(jnp.int32)
        padded = (cnt + rows - 1) // rows * rows
        pad_end = jnp.cumsum(padded)
        pad_start = pad_end - padded
        eid = info[..., 0:2].astype(jnp.int32).reshape(n_tok, TOP_K)
        rank = info[..., 4:6].astype(jnp.int32).reshape(n_tok, TOP_K)
        dest = pad_start[eid] + rank
        slot_tok = jnp.zeros((n_blocks * rows,), jnp.int32).at[dest.reshape(-1)].set(tok_ids)
        block_e = jnp.sum((blk_start[:, None] >= pad_end[None, :]).astype(jnp.int32), axis=1)
        block_e = jnp.minimum(block_e, N_EXPERTS - 1)
        n_used = (pad_end[-1:] // rows).astype(jnp.int32)

        yb = _experts(h.reshape(n_tok, D), slot_tok.reshape(n_blocks, 1, rows), block_e, n_used,
                      w_gate, w_up, w_down, l, rows)
        dest_t = dest.reshape(n_tok // tm, tm, TOP_K).transpose(0, 2, 1).reshape(n_tok // tm, 1, TOP_K * tm)
        xa = _combine_ln(x1, mod, info, dest_t, yb, ln2_g[l][None, :], ln2_b[l][None, :],
                         tm, n_ctx_tiles, alpha, skip_ctx=(l == L - 1))
    return xa
```

```python
import functools

import jax
import jax.numpy as jnp
from jax import lax
from jax.experimental import pallas as pl
from jax.experimental.pallas import tpu as pltpu

GRID_W = 64
ROPE_BASE = 10000.0
RMS_EPS = 1e-6
LN_EPS = 1e-5

MLA_HEADS = 8
MLA_Q_LORA = 384
MLA_KV_LORA = 256
MLA_NOPE = 64
MLA_ROPE = 32
MLA_V = 64
MLA_QK = MLA_NOPE + MLA_ROPE
MLA_SCALE = MLA_QK ** -0.5

HEAD_DIM = 64
GQA_HEADS = 8
GQA_KV_HEADS = 2
GQA_SCALE = HEAD_DIM ** -0.5
LOG2E = 1.4426950408889634

N_GROUPS = 4
EXPERTS_PER_GROUP = 8
N_EXPERTS = N_GROUPS * EXPERTS_PER_GROUP
TOP_K = 2

LANES = 128
MOE_ROWS = 256
VMEM_LIMIT = 56 * 1024 * 1024

_QA0, _KVA0, _KR0, _GQ0, _GK0, _GV0, _INW = 0, 384, 640, 768, 1280, 1408, 1536

_f32 = jnp.float32
_bf16 = jnp.bfloat16


def _cparams(sem):
    return pltpu.CompilerParams(dimension_semantics=sem, vmem_limit_bytes=VMEM_LIMIT)


def _sigmoid(v):
    return 1.0 / (1.0 + jnp.exp(-v))


def _layer_norm(v, g, b):
    mu = jnp.mean(v, axis=-1, keepdims=True)
    d = v - mu
    var = jnp.mean(d * d, axis=-1, keepdims=True)
    return d * lax.rsqrt(var + LN_EPS) * g + b


def _pack_bf16_pairs(v):
    w = v.shape[-1] // 2
    hi = lax.bitcast_convert_type(v[:, :w].astype(_bf16).astype(_f32), jnp.uint32)
    lo = lax.bitcast_convert_type(v[:, w:].astype(_bf16).astype(_f32), jnp.uint32)
    return hi | (lo >> 16)


def _unpack_bf16_pairs(u):
    hi = lax.bitcast_convert_type(u & jnp.uint32(0xFFFF0000), _f32)
    lo = lax.bitcast_convert_type(u << 16, _f32)
    return jnp.concatenate([hi, lo], axis=-1)


def _rot_half(y, dist):
    lane = lax.broadcasted_iota(jnp.int32, y.shape, y.ndim - 1)
    n = y.shape[-1]
    fwd = pltpu.roll(y, n - dist, y.ndim - 1)
    bwd = pltpu.roll(y, dist, y.ndim - 1)
    return jnp.where((lane & dist) == 0, fwd, bwd)


def _adaln_kernel(c_ref, w_ref, b_ref, o_ref):
    cc = c_ref[...]
    s = (cc * _sigmoid(cc)).astype(_bf16)
    o_ref[0] = jnp.dot(s, w_ref[0].astype(_bf16), preferred_element_type=_f32) + b_ref[0]


def _adaln(cc, w_ada, b_ada):
    L, D, N = w_ada.shape
    R = cc.shape[0]
    tn = 1536 if N % 1536 == 0 else N
    return pl.pallas_call(
        _adaln_kernel,
        grid=(L, N // tn),
        in_specs=[pl.BlockSpec((R, D), lambda l, j: (0, 0)),
                  pl.BlockSpec((1, D, tn), lambda l, j: (l, 0, j)),
                  pl.BlockSpec((1, 1, tn), lambda l, j: (l, 0, j))],
        out_specs=pl.BlockSpec((1, R, tn), lambda l, j: (l, 0, j)),
        out_shape=jax.ShapeDtypeStruct((L, R, N), _f32),
        compiler_params=_cparams(("arbitrary", "arbitrary")),
    )(cc, w_ada, b_ada.reshape(L, 1, N))


def _inproj_kernel(x_ref, mod_ref, win_ref, wqb_ref, wkvb_ref, gqa_ref, gkva_ref, hs_ref,
                   aqm_ref, bqm_ref, akm_ref, bkm_ref, aqg_ref, bqg_ref, akg_ref, bkg_ref,
                   qm_ref, km_ref, vm_ref, qg_ref, kg_ref, vg_ref):
    x = x_ref[0]
    h = (x * (1.0 + mod_ref[1:2, :]) + mod_ref[0:1, :]).astype(_bf16)
    z = jnp.dot(h, win_ref[...], preferred_element_type=_f32)

    q_a = z[:, _QA0:_KVA0]
    qn = q_a * lax.rsqrt(jnp.mean(q_a * q_a, axis=-1, keepdims=True) + RMS_EPS) * gqa_ref[...]
    q = jnp.dot(qn.astype(_bf16), wqb_ref[...], preferred_element_type=_f32)
    kv_a = z[:, _KVA0:_KR0]
    kvn = kv_a * lax.rsqrt(jnp.mean(kv_a * kv_a, axis=-1, keepdims=True) + RMS_EPS) * gkva_ref[...]
    kv = jnp.dot(kvn.astype(_bf16), wkvb_ref[...], preferred_element_type=_f32)
    aq, bq = aqm_ref[...], bqm_ref[...]
    for hd in range(MLA_HEADS):
        qh = q[:, hd * LANES:(hd + 1) * LANES]
        qm_ref[0, hd] = (qh * aq + _rot_half(qh, 8) * bq).astype(_bf16)
    kr = z[:, _KR0:_GQ0]
    kr = kr * akm_ref[...] + _rot_half(kr, 8) * bkm_ref[...]
    for hd in range(MLA_HEADS):
        km_ref[0, hd] = (kv[:, hd * LANES:(hd + 1) * LANES] + kr).astype(_bf16)
    v0 = MLA_HEADS * LANES
    ones = jnp.ones((x.shape[0], LANES), _bf16)
    for p in range(MLA_HEADS // 2):
        vm_ref[0, p] = jnp.concatenate([kv[:, v0 + p * LANES:v0 + (p + 1) * LANES].astype(_bf16), ones], axis=-1)

    def head_rs(t, hs):
        sq = t * t
        hi = sq.astype(_bf16)
        lo = (sq - hi.astype(_f32)).astype(_bf16)
        ssum = (jnp.dot(hi, hs, preferred_element_type=_f32)
                + jnp.dot(lo, hs, preferred_element_type=_f32))
        return lax.rsqrt(ssum * (1.0 / HEAD_DIM) + RMS_EPS)

    gq = z[:, _GQ0:_GK0]
    yq = gq * head_rs(gq, hs_ref[...])
    aqg, bqg = aqg_ref[...], bqg_ref[...]
    for j in range(GQA_HEADS // 2):
        yj = yq[:, j * LANES:(j + 1) * LANES]
        qg_ref[0, j] = (yj * aqg + _rot_half(yj, 16) * bqg).astype(_bf16)
    gk = z[:, _GK0:_GV0]
    yk = gk * head_rs(gk, hs_ref[0:LANES, 0:LANES])
    kg_ref[0] = (yk * akg_ref[...] + _rot_half(yk, 16) * bkg_ref[...]).astype(_bf16)
    vg_ref[0] = jnp.concatenate([z[:, _GV0:_INW].astype(_bf16), ones], axis=-1)


def _inproj(xa, mod, win, wqb, wkvb, gqa, gkva, hs, tabs_m, tabs_g, tm, n_ctx_tiles):
    B, T, D = xa.shape
    nt = T // tm
    full = lambda a: pl.BlockSpec(a.shape, lambda b, i: (0,) * a.ndim)
    tab = pl.BlockSpec((tm, LANES), lambda b, i: (i, 0))
    out4 = lambda n, w=LANES: pl.BlockSpec((1, n, tm, w), lambda b, i: (b, 0, i, 0))
    out3 = lambda w=LANES: pl.BlockSpec((1, tm, w), lambda b, i: (b, i, 0))
    sds = lambda *s: jax.ShapeDtypeStruct(s, _bf16)
    return pl.pallas_call(
        _inproj_kernel,
        grid=(B, nt),
        in_specs=[pl.BlockSpec((1, tm, D), lambda b, i: (b, i, 0)),
                  pl.BlockSpec((None, None, 6, D), lambda b, i: (b, jnp.where(i >= n_ctx_tiles, 1, 0), 0, 0)),
                  full(win), full(wqb), full(wkvb), full(gqa), full(gkva), full(hs),
                  tab, tab, tab, tab, tab, tab, tab, tab],
        out_specs=[out4(8), out4(8), out4(4, 2 * LANES), out4(4), out3(), out3(2 * LANES)],
        out_shape=[sds(B, 8, T, LANES), sds(B, 8, T, LANES), sds(B, 4, T, 2 * LANES),
                   sds(B, 4, T, LANES), sds(B, T, LANES), sds(B, T, 2 * LANES)],
        compiler_params=_cparams(("arbitrary", "arbitrary")),
    )(xa, mod, win, wqb, wkvb, gqa, gkva, hs, *tabs_m, *tabs_g)


def _softmax_pv(q, k, v_ext):
    s = lax.dot_general(q, k, (((1,), (1,)), ((), ())), preferred_element_type=_f32)
    m = jnp.max(s, axis=-1, keepdims=True)
    p = jnp.exp2((s - m).astype(_bf16))
    pv = jnp.dot(p, v_ext, preferred_element_type=_f32)
    return pv[:, 0:LANES] * (1.0 / pv[:, LANES:LANES + 1])


def _attn_kernel(qm_ref, km_ref, vm_ref, qg_ref, kg_ref, vg_ref, o_ref, *, n_ctx, n_ctx_tiles):
    tq = qm_ref.shape[2]
    n_all = km_ref.shape[2]
    lane = lax.broadcasted_iota(jnp.int32, (tq, LANES), 1)
    low = lane < (LANES // 2)

    def run(klen):
        def mla_pair(p, carry):
            v = vm_ref[0, p, 0:klen, :]
            o0 = _softmax_pv(qm_ref[0, 2 * p], km_ref[0, 2 * p, 0:klen, :], v)
            o1 = _softmax_pv(qm_ref[0, 2 * p + 1], km_ref[0, 2 * p + 1, 0:klen, :], v)
            o_ref[0, p] = jnp.where(low, o0, o1).astype(_bf16)
            return carry

        lax.fori_loop(0, MLA_HEADS // 2, mla_pair, 0, unroll=True)

        def gqa_slab(j, carry):
            qs = qg_ref[0, j]
            k = kg_ref[0, 0:klen, :]
            v = vg_ref[0, 0:klen, :]
            o0 = _softmax_pv(jnp.where(low, qs, jnp.zeros_like(qs)), k, v)
            o1 = _softmax_pv(jnp.where(low, jnp.zeros_like(qs), qs), k, v)
            o_ref[0, MLA_HEADS // 2 + j] = jnp.where(low, o0, o1).astype(_bf16)
            return carry

        lax.fori_loop(0, GQA_HEADS // 2, gqa_slab, 0, unroll=True)

    i = pl.program_id(1)

    @pl.when(i < n_ctx_tiles)
    def _():
        run(n_ctx)

    @pl.when(i >= n_ctx_tiles)
    def _():
        run(n_all)


def _attention(qm, km, vm, qg, kg, vg, tq, n_ctx):
    B, _, T, _ = qm.shape
    qspec = lambda n: pl.BlockSpec((1, n, tq, LANES), lambda b, i: (b, 0, i, 0))
    kspec = lambda n, w=LANES: pl.BlockSpec((1, n, T, w), lambda b, i: (b, 0, 0, 0))
    k3 = lambda w=LANES: pl.BlockSpec((1, T, w), lambda b, i: (b, 0, 0))
    return pl.pallas_call(
        functools.partial(_attn_kernel, n_ctx=n_ctx, n_ctx_tiles=n_ctx // tq),
        grid=(B, T // tq),
        in_specs=[qspec(8), kspec(8), kspec(4, 2 * LANES), qspec(4), k3(), k3(2 * LANES)],
        out_specs=qspec(8),
        out_shape=jax.ShapeDtypeStruct((B, 8, T, LANES), _bf16),
        compiler_params=_cparams(("arbitrary", "arbitrary")),
    )(qm, km, vm, qg, kg, vg)


def _wo_router_kernel(x_ref, o_ref, mod_ref, wo_ref, g_ref, b_ref, wr_ref, br_ref,
                      x1_ref, h_ref, info_ref, cnt_ref, carry_ref, *, alpha):
    tm = x_ref.shape[1]
    first = (pl.program_id(0) == 0) & (pl.program_id(1) == 0)

    @pl.when(first)
    def _():
        carry_ref[...] = jnp.zeros_like(carry_ref)

    o = jnp.concatenate([o_ref[0, p] for p in range(o_ref.shape[1])], axis=-1)
    mix = jnp.dot(o, wo_ref[...], preferred_element_type=_f32)
    x1 = _layer_norm(alpha * x_ref[0] + mod_ref[2:3, :] * mix, g_ref[...], b_ref[...])
    x1_ref[0] = x1

    h = x1 * (1.0 + mod_ref[4:5, :]) + mod_ref[3:4, :]
    h_ref[0] = _pack_bf16_pairs(h)
    logits = jnp.dot(h.astype(_bf16), wr_ref[...], preferred_element_type=_f32) + br_ref[...]
    lane_i = lax.broadcasted_iota(jnp.int32, (tm, LANES), 1)
    lane = lane_i.astype(_f32)
    neg = jnp.float32(-jnp.inf)
    big = jnp.float32(LANES)

    def arg_first_max(vals):
        mx = jnp.max(vals, axis=-1, keepdims=True)
        idx = jnp.min(jnp.where(vals == mx, lane, big), axis=-1, keepdims=True)
        return mx, idx

    is_grp = (lane_i >= N_EXPERTS) & (lane_i < N_EXPERTS + N_GROUPS)
    lg = jnp.where(is_grp, logits, neg)
    gmax, gidx = arg_first_max(lg)
    p_star = 1.0 / jnp.sum(jnp.exp(lg - gmax), axis=-1, keepdims=True)
    g_star = gidx - N_EXPERTS
    in_grp = (lane_i < N_EXPERTS) & ((lane_i >> 3).astype(_f32) == g_star)
    le = jnp.where(in_grp, logits, neg)
    v1, e1 = arg_first_max(le)
    v2, e2 = arg_first_max(jnp.where(lane == e1, neg, le))
    t = jnp.exp(v2 - v1)
    w1 = p_star / (1.0 + t)
    w2 = p_star * t / (1.0 + t)

    onehot = ((lane == e1) | (lane == e2)).astype(_f32)
    r = lax.broadcasted_iota(jnp.int32, (tm, tm), 0)
    c = lax.broadcasted_iota(jnp.int32, (tm, tm), 1)
    tri = (c < r).astype(_bf16)
    prefix = jnp.dot(tri, onehot.astype(_bf16), preferred_element_type=_f32) + carry_ref[...]
    r1 = jnp.sum(jnp.where(lane == e1, prefix, 0.0), axis=-1, keepdims=True)
    r2 = jnp.sum(jnp.where(lane == e2, prefix, 0.0), axis=-1, keepdims=True)
    carry_ref[...] += jnp.sum(onehot, axis=0, keepdims=True)
    cnt_ref[...] = carry_ref[...]

    l8 = lax.broadcasted_iota(jnp.int32, (tm, 8), 1)
    cols = [e1, e2, w1, w2, r1, r2]
    info = jnp.zeros((tm, 8), _f32)
    for k, col in enumerate(cols):
        info = jnp.where(l8 == k, col, info)
    info_ref[0] = info


def _wo_router(xa, o, mod, wo, g, b, wr, br, tm, n_ctx_tiles, alpha):
    B, T, D = xa.shape
    full = lambda a: pl.BlockSpec(a.shape, lambda bb, i: (0,) * a.ndim)
    row = pl.BlockSpec((1, tm, D), lambda bb, i: (bb, i, 0))
    return pl.pallas_call(
        functools.partial(_wo_router_kernel, alpha=alpha),
        grid=(B, T // tm),
        in_specs=[row,
                  pl.BlockSpec((1, 8, tm, LANES), lambda bb, i: (bb, 0, i, 0)),
                  pl.BlockSpec((None, None, 6, D), lambda bb, i: (bb, jnp.where(i >= n_ctx_tiles, 1, 0), 0, 0)),
                  full(wo), full(g), full(b), full(wr), full(br)],
        out_specs=[row, pl.BlockSpec((1, tm, D // 2), lambda bb, i: (bb, i, 0)),
                   pl.BlockSpec((1, tm, 8), lambda bb, i: (bb, i, 0)),
                   pl.BlockSpec((1, LANES), lambda bb, i: (0, 0))],
        out_shape=[jax.ShapeDtypeStruct((B, T, D), _f32),
                   jax.ShapeDtypeStruct((B, T, D // 2), jnp.uint32),
                   jax.ShapeDtypeStruct((B, T, 8), _f32),
                   jax.ShapeDtypeStruct((1, LANES), _f32)],
        scratch_shapes=[pltpu.VMEM((1, LANES), _f32)],
        compiler_params=_cparams(("arbitrary", "arbitrary")),
    )(xa, o, mod, wo, g, b, wr, br)


def _row_copy(src_ref, row, dst_ref, slot, sem):
    return pltpu.make_async_copy(src_ref.at[pl.ds(row, 1), :], dst_ref.at[pl.ds(slot, 1), :], sem)


def _issue_rows(src_ref, idx_ref, dst_ref, n, sem):
    for r in range(n):
        _row_copy(src_ref, idx_ref[0, 0, r], dst_ref, r, sem).start()


def _wait_rows(src_ref, dst_ref, n, sem):
    for r in range(n):
        _row_copy(src_ref, 0, dst_ref, r, sem).wait()


def _expert_kernel(be_ref, nu_ref, tok0_ref, tok1_ref, tok2_ref, h_ref, wg_ref, wu_ref, wd_ref, y_ref,
                   xbuf0, xbuf1, xbuf2, wg_bf, wu_bf, wd_bf, sem):
    i = pl.program_id(0)
    rows = xbuf0.shape[0]
    n_used = nu_ref[0]
    active = i < n_used
    draining = (i == n_used) | (i == n_used + 1)
    phase = i % 3
    bufs = (xbuf0, xbuf1, xbuf2)

    @pl.when(i == 0)
    def _():
        _issue_rows(h_ref, tok0_ref, xbuf0, rows, sem.at[0])
        _issue_rows(h_ref, tok1_ref, xbuf1, rows, sem.at[1])

    new_expert = active & ((i == 0) | (be_ref[i] != be_ref[jnp.maximum(i - 1, 0)]))

    @pl.when(new_expert)
    def _():
        wg_bf[...] = wg_ref[0, 0].astype(_bf16)
        wu_bf[...] = wu_ref[0, 0].astype(_bf16)
        wd_bf[...] = wd_ref[0, 0].astype(_bf16)

    def step(p):
        q = (p + 2) % 3
        _wait_rows(h_ref, bufs[p], rows, sem.at[p])
        xb = _unpack_bf16_pairs(bufs[p][...]).astype(_bf16)
        _issue_rows(h_ref, tok2_ref, bufs[q], rows, sem.at[q])
        g = jnp.dot(xb, wg_bf[...], preferred_element_type=_f32)
        u = jnp.dot(xb, wu_bf[...], preferred_element_type=_f32)
        a = (g * _sigmoid(g) * u).astype(_bf16)
        y_ref[...] = _pack_bf16_pairs(jnp.dot(a, wd_bf[...], preferred_element_type=_f32))

    for p in range(3):
        pl.when(active & (phase == p))(functools.partial(step, p))
        pl.when(draining & (phase == p))(
            functools.partial(_wait_rows, h_ref, bufs[p], rows, sem.at[p]))

    @pl.when(jnp.logical_not(active))
    def _():
        y_ref[...] = jnp.zeros_like(y_ref)


def _experts(h2d, slot_tok, block_e, n_used, w_gate, w_up, w_down, layer, rows):
    n_blocks = slot_tok.shape[0]
    _, _, D, F = w_gate.shape
    tok = lambda k: pl.BlockSpec((1, 1, rows), lambda i, be, nu: (jnp.minimum(i + k, n_blocks - 1), 0, 0),
                                 memory_space=pltpu.SMEM)
    grid_spec = pltpu.PrefetchScalarGridSpec(
        num_scalar_prefetch=2,
        grid=(n_blocks,),
        in_specs=[tok(0), tok(1), tok(2),
                  pl.BlockSpec(memory_space=pl.ANY),
                  pl.BlockSpec((1, 1, D, F), lambda i, be, nu: (layer, be[i], 0, 0)),
                  pl.BlockSpec((1, 1, D, F), lambda i, be, nu: (layer, be[i], 0, 0)),
                  pl.BlockSpec((1, 1, F, D), lambda i, be, nu: (layer, be[i], 0, 0))],
        out_specs=pl.BlockSpec((rows, D // 2), lambda i, be, nu: (i, 0)),
        scratch_shapes=[pltpu.VMEM((rows, D // 2), jnp.uint32), pltpu.VMEM((rows, D // 2), jnp.uint32),
                        pltpu.VMEM((rows, D // 2), jnp.uint32),
                        pltpu.VMEM((D, F), _bf16), pltpu.VMEM((D, F), _bf16), pltpu.VMEM((F, D), _bf16),
                        pltpu.SemaphoreType.DMA((3,))],
    )
    return pl.pallas_call(
        _expert_kernel,
        grid_spec=grid_spec,
        out_shape=jax.ShapeDtypeStruct((n_blocks * rows, D // 2), jnp.uint32),
        compiler_params=_cparams(("arbitrary",)),
    )(block_e, n_used, slot_tok, slot_tok, slot_tok, h2d, w_gate, w_up, w_down)


def _combine_kernel(dest0_ref, dest1_ref, dest2_ref, x_ref, mod_ref, info_ref, yb_ref, g_ref, b_ref, out_ref,
                    ybuf0, ybuf1, ybuf2, sem, *, alpha, n_tiles):
    tm = x_ref.shape[0]
    g_id = pl.program_id(0)
    active = g_id < n_tiles
    phase = g_id % 3
    bufs = (ybuf0, ybuf1, ybuf2)

    @pl.when(g_id == 0)
    def _():
        _issue_rows(yb_ref, dest0_ref, ybuf0, 2 * tm, sem.at[0])
        _issue_rows(yb_ref, dest1_ref, ybuf1, 2 * tm, sem.at[1])

    def step(p):
        q = (p + 2) % 3
        buf = bufs[p]
        _wait_rows(yb_ref, buf, 2 * tm, sem.at[p])
        _issue_rows(yb_ref, dest2_ref, bufs[q], 2 * tm, sem.at[q])
        info = info_ref[...]
        y = (info[:, 2:3] * _unpack_bf16_pairs(buf[0:tm, :])
             + info[:, 3:4] * _unpack_bf16_pairs(buf[tm:2 * tm, :]))
        out_ref[...] = _layer_norm(alpha * x_ref[...] + mod_ref[5:6, :] * y, g_ref[...], b_ref[...])

    for p in range(3):
        pl.when(active & (phase == p))(functools.partial(step, p))
        pl.when(jnp.logical_not(active) & (phase == p))(
            functools.partial(_wait_rows, yb_ref, bufs[p], 2 * tm, sem.at[p]))


def _combine_ln(x1, mod, info, dest, yb, g, b, tm, n_ctx_tiles, alpha, skip_ctx):
    B, T, D = x1.shape
    nt = T // tm
    off = n_ctx_tiles if skip_ctx else 0
    ntl = nt - off
    n_tiles = B * ntl
    clamp = lambda s: jnp.minimum(s, n_tiles - 1)
    tile = lambda s: (s // ntl) * nt + off + s % ntl
    full = lambda a: pl.BlockSpec(a.shape, lambda s: (0,) * a.ndim)
    dspec = lambda k: pl.BlockSpec((1, 1, 2 * tm), lambda s: (tile(clamp(s + k)), 0, 0), memory_space=pltpu.SMEM)
    out = pl.pallas_call(
        functools.partial(_combine_kernel, alpha=alpha, n_tiles=n_tiles),
        grid=(n_tiles + 2,),
        in_specs=[dspec(0), dspec(1), dspec(2),
                  pl.BlockSpec((tm, D), lambda s: (tile(clamp(s)), 0)),
                  pl.BlockSpec((None, None, 6, D),
                               lambda s: (clamp(s) // ntl, jnp.where(clamp(s) % ntl + off >= n_ctx_tiles, 1, 0), 0, 0)),
                  pl.BlockSpec((tm, 8), lambda s: (tile(clamp(s)), 0)),
                  pl.BlockSpec(memory_space=pl.ANY),
                  full(g), full(b)],
        out_specs=pl.BlockSpec((tm, D), lambda s: (clamp(s), 0)),
        out_shape=jax.ShapeDtypeStruct((n_tiles * tm, D), _f32),
        scratch_shapes=[pltpu.VMEM((2 * tm, D // 2), jnp.uint32), pltpu.VMEM((2 * tm, D // 2), jnp.uint32),
                        pltpu.VMEM((2 * tm, D // 2), jnp.uint32), pltpu.SemaphoreType.DMA((3,))],
        compiler_params=_cparams(("arbitrary",)),
    )(dest, dest, dest, x1.reshape(B * T, D), mod, info.reshape(B * T, 8), yb, g, b)
    return out.reshape(B, ntl * tm, D)


def _prep_weights(w_in, w_q_b, w_kv_b, w_o, w_router_grp, b_router_grp, w_router_exp, b_router_exp):
    L, D, _ = w_in.shape
    zeros = lambda *s: jnp.zeros((L,) + s, _f32)
    kr = jnp.concatenate([zeros(D, 64), w_in[:, :, 640:672], zeros(D, 32)], axis=-1)
    gq = w_in[:, :, 672:1184].reshape(L, D, GQA_HEADS, HEAD_DIM)
    gq = jnp.stack([gq[:, :, :4], gq[:, :, 4:]], axis=3).reshape(L, D, 512)
    win = jnp.concatenate([w_in[:, :, 0:640], kr, gq, w_in[:, :, 1184:1440]], axis=-1).astype(_bf16)

    qb = w_q_b.reshape(L, MLA_Q_LORA, MLA_HEADS, MLA_QK)
    wqb = jnp.concatenate([qb, zeros(MLA_Q_LORA, MLA_HEADS, LANES - MLA_QK)], axis=-1)
    wqb = wqb.reshape(L, MLA_Q_LORA, MLA_HEADS * LANES).astype(_bf16)

    kvb = w_kv_b.reshape(L, MLA_KV_LORA, MLA_HEADS, MLA_NOPE + MLA_V)
    kpart = jnp.concatenate([kvb[..., :MLA_NOPE], zeros(MLA_KV_LORA, MLA_HEADS, LANES - MLA_NOPE)], axis=-1)
    vpart = kvb[..., MLA_NOPE:].reshape(L, MLA_KV_LORA, MLA_HEADS * MLA_V)
    wkvb = jnp.concatenate([kpart.reshape(L, MLA_KV_LORA, MLA_HEADS * LANES), vpart], axis=-1).astype(_bf16)

    n_mla = MLA_HEADS * MLA_V
    og = w_o[:, n_mla:, :].reshape(L, GQA_HEADS, HEAD_DIM, D)
    og = jnp.stack([og[:, :4], og[:, 4:]], axis=2).reshape(L, GQA_HEADS * HEAD_DIM, D)
    wo = jnp.concatenate([w_o[:, :n_mla, :], og], axis=1).astype(_bf16)

    pad = LANES - N_EXPERTS - N_GROUPS
    wr = jnp.concatenate([w_router_exp, w_router_grp, zeros(D, pad)], axis=-1).astype(_bf16)
    br = jnp.concatenate([b_router_exp, b_router_grp, jnp.zeros((L, pad), _f32)], axis=-1).reshape(L, 1, LANES)
    return win, wqb, wkvb, wo, wr, br


def _rope_tables(S, C, g_gqa_q, g_gqa_k):
    pos = jnp.arange(S)
    row = (pos // GRID_W).astype(_f32)
    col = (pos % GRID_W).astype(_f32)

    def angles(half):
        inv = ROPE_BASE ** (-(jnp.arange(0, half, 2, dtype=_f32) / half))
        ar, ac = row[:, None] * inv, col[:, None] * inv
        return jnp.concatenate([ar, ar, ac, ac], axis=-1)

    def cos_sin(half):
        ang = angles(half)
        sign = jnp.tile(jnp.concatenate([-jnp.ones(half // 2), jnp.ones(half // 2)]), 2).astype(_f32)
        cos = jnp.concatenate([jnp.ones((C, 2 * half), _f32), jnp.cos(ang)], axis=0)
        sin = jnp.concatenate([jnp.zeros((C, 2 * half), _f32), jnp.sin(ang) * sign], axis=0)
        return cos, sin

    T = S + C
    cm, sm = cos_sin(MLA_ROPE // 2)
    a_m = jnp.concatenate([jnp.ones((T, 64), _f32), cm, jnp.ones((T, 32), _f32)], axis=-1)
    b_m = jnp.concatenate([jnp.zeros((T, 64), _f32), sm, jnp.zeros((T, 32), _f32)], axis=-1)
    tabs_m = (a_m * (MLA_SCALE * LOG2E), b_m * (MLA_SCALE * LOG2E), a_m, b_m)

    cg, sg = cos_sin(HEAD_DIM // 2)
    swap = jnp.arange(HEAD_DIM) ^ (HEAD_DIM // 4)

    def gqa(g, scale):
        a = g[:, None, :] * cg[None] * scale
        b = g[:, swap][:, None, :] * sg[None] * scale
        return jnp.tile(a, (1, 1, 2)), jnp.tile(b, (1, 1, 2))

    aq, bq = gqa(g_gqa_q, GQA_SCALE * LOG2E)
    ak, bk = gqa(g_gqa_k, 1.0)
    return tabs_m, (aq, bq, ak, bk)


def kernel(x, c, ctx, c_ctx, w_ada, b_ada, w_in, g_q_a, w_q_b, g_kv_a, w_kv_b, g_gqa_q, g_gqa_k,
           w_o, ln1_g, ln1_b, w_router_grp, b_router_grp, w_router_exp, b_router_exp,
           w_gate, w_up, w_down, ln2_g, ln2_b):
    B, S, D = x.shape
    C = ctx.shape[1]
    L = w_in.shape[0]
    T = C + S
    alpha = (2.0 * L) ** 0.25
    tm = 256 if C % 256 == 0 else 128
    assert C % tm == 0 and S % tm == 0
    n_ctx_tiles = C // tm
    rows = MOE_ROWS

    R = -(-(B + 1) // 8) * 8
    cc = jnp.concatenate([c, c_ctx[None, :], jnp.zeros((R - B - 1, D), _f32)], axis=0)
    mods = _adaln(cc, w_ada, b_ada)
    lat = mods[:, :B].reshape(L, B, 1, 6, D)
    cx = jnp.broadcast_to(mods[:, B].reshape(L, 1, 1, 6, D), (L, B, 1, 6, D))
    mods = jnp.concatenate([cx, lat], axis=2)

    win, wqb, wkvb, wo, wr, br = _prep_weights(w_in, w_q_b, w_kv_b, w_o, w_router_grp, b_router_grp,
                                               w_router_exp, b_router_exp)
    tabs_m, tabs_g = _rope_tables(S, C, g_gqa_q, g_gqa_k)
    blk = jnp.arange(GQA_HEADS * HEAD_DIM) // HEAD_DIM
    hs = (blk[:, None] == blk[None, :]).astype(_bf16)

    n_tok = B * T
    n_assign = n_tok * TOP_K
    n_blocks = -(-n_assign // rows) + N_EXPERTS + 2
    tok_ids = jnp.repeat(jnp.arange(n_tok, dtype=jnp.int32), TOP_K)
    blk_start = jnp.arange(n_blocks, dtype=jnp.int32) * rows

    xa = jnp.concatenate([ctx, x], axis=1)
    for l in range(L):
        mod = mods[l]
        qm, km, vm, qg, kg, vg = _inproj(
            xa, mod, win[l], wqb[l], wkvb[l], g_q_a[l][None, :], g_kv_a[l][None, :], hs,
            tabs_m, tuple(t[l] for t in tabs_g), tm, n_ctx_tiles)
        o = _attention(qm, km, vm, qg, kg, vg, tm, C)
        x1, h, info, counts = _wo_router(xa, o, mod, wo[l], ln1_g[l][None, :], ln1_b[l][None, :],
                                         wr[l], br[l], tm, n_ctx_tiles, alpha)
        cnt = counts[0, :N_EXPERTS].astype(jnp.int32)
        padded = (cnt + rows - 1) // rows * rows
        pad_end = jnp.cumsum(padded)
        pad_start = pad_end - padded
        eid = info[..., 0:2].astype(jnp.int32).reshape(n_tok, TOP_K)
        rank = info[..., 4:6].astype(jnp.int32).reshape(n_tok, TOP_K)
        dest = pad_start[eid] + rank
        slot_tok = jnp.zeros((n_blocks * rows,), jnp.int32).at[dest.reshape(-1)].set(tok_ids)
        block_e = jnp.sum((blk_start[:, None] >= pad_end[None, :]).astype(jnp.int32), axis=1)
        block_e = jnp.minimum(block_e, N_EXPERTS - 1)
        n_used = (pad_end[-1:] // rows).astype(jnp.int32)

        yb = _experts(h.reshape(n_tok, D // 2), slot_tok.reshape(n_blocks, 1, rows), block_e, n_used,
                      w_gate, w_up, w_down, l, rows)
        dest_t = dest.reshape(n_tok // tm, tm, TOP_K).transpose(0, 2, 1).reshape(n_tok // tm, 1, TOP_K * tm)
        xa = _combine_ln(x1, mod, info, dest_t, yb, ln2_g[l][None, :], ln2_b[l][None, :],
                         tm, n_ctx_tiles, alpha, skip_ctx=(l == L - 1))
    return xa
```

```python
import functools

import jax
import jax.numpy as jnp
from jax import lax
from jax.experimental import pallas as pl
from jax.experimental.pallas import tpu as pltpu

GRID_W = 64
ROPE_BASE = 10000.0
RMS_EPS = 1e-6
LN_EPS = 1e-5

MLA_HEADS = 8
MLA_Q_LORA = 384
MLA_KV_LORA = 256
MLA_NOPE = 64
MLA_ROPE = 32
MLA_V = 64
MLA_QK = MLA_NOPE + MLA_ROPE
MLA_SCALE = MLA_QK ** -0.5

HEAD_DIM = 64
GQA_HEADS = 8
GQA_KV_HEADS = 2
GQA_SCALE = HEAD_DIM ** -0.5
LOG2E = 1.4426950408889634

N_GROUPS = 4
EXPERTS_PER_GROUP = 8
N_EXPERTS = N_GROUPS * EXPERTS_PER_GROUP
TOP_K = 2

LANES = 128
MOE_ROWS = 256
VMEM_LIMIT = 56 * 1024 * 1024

_QA0, _KVA0, _KR0, _GQ0, _GK0, _GV0, _INW = 0, 384, 640, 768, 1280, 1408, 1536

_f32 = jnp.float32
_bf16 = jnp.bfloat16


def _cparams(sem):
    return pltpu.CompilerParams(dimension_semantics=sem, vmem_limit_bytes=VMEM_LIMIT)


def _sigmoid(v):
    return 1.0 / (1.0 + jnp.exp(-v))


def _layer_norm(v, g, b):
    mu = jnp.mean(v, axis=-1, keepdims=True)
    d = v - mu
    var = jnp.mean(d * d, axis=-1, keepdims=True)
    return d * lax.rsqrt(var + LN_EPS) * g + b


def _pack_bf16_pairs(v):
    w = v.shape[-1] // 2
    hi = lax.bitcast_convert_type(v[:, :w].astype(_bf16).astype(_f32), jnp.uint32)
    lo = lax.bitcast_convert_type(v[:, w:].astype(_bf16).astype(_f32), jnp.uint32)
    return hi | (lo >> 16)


def _unpack_bf16_pairs(u):
    hi = lax.bitcast_convert_type(u & jnp.uint32(0xFFFF0000), _f32)
    lo = lax.bitcast_convert_type(u << 16, _f32)
    return jnp.concatenate([hi, lo], axis=-1)


def _store_row_tiles(ref, idx, packed):
    for j in range(packed.shape[-1] // LANES):
        ref[idx + (slice(None), j, slice(None))] = packed[:, j * LANES:(j + 1) * LANES]


def _load_row_tiles(ref, lo, hi):
    return jnp.concatenate([ref[lo:hi, j, :] for j in range(ref.shape[1])], axis=-1)


def _rot_half(y, dist):
    lane = lax.broadcasted_iota(jnp.int32, y.shape, y.ndim - 1)
    n = y.shape[-1]
    fwd = pltpu.roll(y, n - dist, y.ndim - 1)
    bwd = pltpu.roll(y, dist, y.ndim - 1)
    return jnp.where((lane & dist) == 0, fwd, bwd)


def _adaln_kernel(c_ref, w_ref, b_ref, o_ref):
    cc = c_ref[...]
    s = (cc * _sigmoid(cc)).astype(_bf16)
    o_ref[0] = jnp.dot(s, w_ref[0].astype(_bf16), preferred_element_type=_f32) + b_ref[0]


def _adaln(cc, w_ada, b_ada):
    L, D, N = w_ada.shape
    R = cc.shape[0]
    tn = 1536 if N % 1536 == 0 else N
    return pl.pallas_call(
        _adaln_kernel,
        grid=(L, N // tn),
        in_specs=[pl.BlockSpec((R, D), lambda l, j: (0, 0)),
                  pl.BlockSpec((1, D, tn), lambda l, j: (l, 0, j)),
                  pl.BlockSpec((1, 1, tn), lambda l, j: (l, 0, j))],
        out_specs=pl.BlockSpec((1, R, tn), lambda l, j: (l, 0, j)),
        out_shape=jax.ShapeDtypeStruct((L, R, N), _f32),
        compiler_params=_cparams(("arbitrary", "arbitrary")),
    )(cc, w_ada, b_ada.reshape(L, 1, N))


def _inproj_kernel(x_ref, mod_ref, win_ref, wqb_ref, wkvb_ref, gqa_ref, gkva_ref, hs_ref,
                   aqm_ref, bqm_ref, akm_ref, bkm_ref, aqg_ref, bqg_ref, akg_ref, bkg_ref,
                   qm_ref, km_ref, vm_ref, qg_ref, kg_ref, vg_ref):
    x = x_ref[0]
    h = (x * (1.0 + mod_ref[1:2, :]) + mod_ref[0:1, :]).astype(_bf16)
    z = jnp.dot(h, win_ref[...], preferred_element_type=_f32)

    q_a = z[:, _QA0:_KVA0]
    qn = q_a * lax.rsqrt(jnp.mean(q_a * q_a, axis=-1, keepdims=True) + RMS_EPS) * gqa_ref[...]
    q = jnp.dot(qn.astype(_bf16), wqb_ref[...], preferred_element_type=_f32)
    kv_a = z[:, _KVA0:_KR0]
    kvn = kv_a * lax.rsqrt(jnp.mean(kv_a * kv_a, axis=-1, keepdims=True) + RMS_EPS) * gkva_ref[...]
    kv = jnp.dot(kvn.astype(_bf16), wkvb_ref[...], preferred_element_type=_f32)
    aq, bq = aqm_ref[...], bqm_ref[...]
    for hd in range(MLA_HEADS):
        qh = q[:, hd * LANES:(hd + 1) * LANES]
        qm_ref[0, hd] = (qh * aq + _rot_half(qh, 8) * bq).astype(_bf16)
    kr = z[:, _KR0:_GQ0]
    kr = kr * akm_ref[...] + _rot_half(kr, 8) * bkm_ref[...]
    for hd in range(MLA_HEADS):
        km_ref[0, hd] = (kv[:, hd * LANES:(hd + 1) * LANES] + kr).astype(_bf16)
    v0 = MLA_HEADS * LANES
    ones = jnp.ones((x.shape[0], LANES), _bf16)
    for p in range(MLA_HEADS // 2):
        vm_ref[0, p] = jnp.concatenate([kv[:, v0 + p * LANES:v0 + (p + 1) * LANES].astype(_bf16), ones], axis=-1)

    def head_rs(t, hs):
        sq = t * t
        hi = sq.astype(_bf16)
        lo = (sq - hi.astype(_f32)).astype(_bf16)
        ssum = (jnp.dot(hi, hs, preferred_element_type=_f32)
                + jnp.dot(lo, hs, preferred_element_type=_f32))
        return lax.rsqrt(ssum * (1.0 / HEAD_DIM) + RMS_EPS)

    gq = z[:, _GQ0:_GK0]
    yq = gq * head_rs(gq, hs_ref[...])
    aqg, bqg = aqg_ref[...], bqg_ref[...]
    for j in range(GQA_HEADS // 2):
        yj = yq[:, j * LANES:(j + 1) * LANES]
        qg_ref[0, j] = (yj * aqg + _rot_half(yj, 16) * bqg).astype(_bf16)
    gk = z[:, _GK0:_GV0]
    yk = gk * head_rs(gk, hs_ref[0:LANES, 0:LANES])
    kg_ref[0] = (yk * akg_ref[...] + _rot_half(yk, 16) * bkg_ref[...]).astype(_bf16)
    vg_ref[0] = jnp.concatenate([z[:, _GV0:_INW].astype(_bf16), ones], axis=-1)


def _inproj(xa, mod, win, wqb, wkvb, gqa, gkva, hs, tabs_m, tabs_g, tm, n_ctx_tiles):
    B, T, D = xa.shape
    nt = T // tm
    full = lambda a: pl.BlockSpec(a.shape, lambda b, i: (0,) * a.ndim)
    tab = pl.BlockSpec((tm, LANES), lambda b, i: (i, 0))
    out4 = lambda n, w=LANES: pl.BlockSpec((1, n, tm, w), lambda b, i: (b, 0, i, 0))
    out3 = lambda w=LANES: pl.BlockSpec((1, tm, w), lambda b, i: (b, i, 0))
    sds = lambda *s: jax.ShapeDtypeStruct(s, _bf16)
    return pl.pallas_call(
        _inproj_kernel,
        grid=(B, nt),
        in_specs=[pl.BlockSpec((1, tm, D), lambda b, i: (b, i, 0)),
                  pl.BlockSpec((None, None, 6, D), lambda b, i: (b, jnp.where(i >= n_ctx_tiles, 1, 0), 0, 0)),
                  full(win), full(wqb), full(wkvb), full(gqa), full(gkva), full(hs),
                  tab, tab, tab, tab, tab, tab, tab, tab],
        out_specs=[out4(8), out4(8), out4(4, 2 * LANES), out4(4), out3(), out3(2 * LANES)],
        out_shape=[sds(B, 8, T, LANES), sds(B, 8, T, LANES), sds(B, 4, T, 2 * LANES),
                   sds(B, 4, T, LANES), sds(B, T, LANES), sds(B, T, 2 * LANES)],
        compiler_params=_cparams(("arbitrary", "arbitrary")),
    )(xa, mod, win, wqb, wkvb, gqa, gkva, hs, *tabs_m, *tabs_g)


def _softmax_pv(q, k, v_ext):
    s = lax.dot_general(q, k, (((1,), (1,)), ((), ())), preferred_element_type=_f32)
    m = jnp.max(s, axis=-1, keepdims=True)
    p = jnp.exp2((s - m).astype(_bf16))
    pv = jnp.dot(p, v_ext, preferred_element_type=_f32)
    return pv[:, 0:LANES] * (1.0 / pv[:, LANES:LANES + 1])


def _attn_kernel(qm_ref, km_ref, vm_ref, qg_ref, kg_ref, vg_ref, o_ref, *, n_ctx, n_ctx_tiles):
    tq = qm_ref.shape[2]
    n_all = km_ref.shape[2]
    lane = lax.broadcasted_iota(jnp.int32, (tq, LANES), 1)
    low = lane < (LANES // 2)

    def run(klen):
        def mla_pair(p, carry):
            v = vm_ref[0, p, 0:klen, :]
            o0 = _softmax_pv(qm_ref[0, 2 * p], km_ref[0, 2 * p, 0:klen, :], v)
            o1 = _softmax_pv(qm_ref[0, 2 * p + 1], km_ref[0, 2 * p + 1, 0:klen, :], v)
            o_ref[0, p] = jnp.where(low, o0, o1).astype(_bf16)
            return carry

        lax.fori_loop(0, MLA_HEADS // 2, mla_pair, 0, unroll=True)

        def gqa_slab(j, carry):
            qs = qg_ref[0, j]
            k = kg_ref[0, 0:klen, :]
            v = vg_ref[0, 0:klen, :]
            o0 = _softmax_pv(jnp.where(low, qs, jnp.zeros_like(qs)), k, v)
            o1 = _softmax_pv(jnp.where(low, jnp.zeros_like(qs), qs), k, v)
            o_ref[0, MLA_HEADS // 2 + j] = jnp.where(low, o0, o1).astype(_bf16)
            return carry

        lax.fori_loop(0, GQA_HEADS // 2, gqa_slab, 0, unroll=True)

    i = pl.program_id(1)

    @pl.when(i < n_ctx_tiles)
    def _():
        run(n_ctx)

    @pl.when(i >= n_ctx_tiles)
    def _():
        run(n_all)


def _attention(qm, km, vm, qg, kg, vg, tq, n_ctx):
    B, _, T, _ = qm.shape
    qspec = lambda n: pl.BlockSpec((1, n, tq, LANES), lambda b, i: (b, 0, i, 0))
    kspec = lambda n, w=LANES: pl.BlockSpec((1, n, T, w), lambda b, i: (b, 0, 0, 0))
    k3 = lambda w=LANES: pl.BlockSpec((1, T, w), lambda b, i: (b, 0, 0))
    return pl.pallas_call(
        functools.partial(_attn_kernel, n_ctx=n_ctx, n_ctx_tiles=n_ctx // tq),
        grid=(B, T // tq),
        in_specs=[qspec(8), kspec(8), kspec(4, 2 * LANES), qspec(4), k3(), k3(2 * LANES)],
        out_specs=qspec(8),
        out_shape=jax.ShapeDtypeStruct((B, 8, T, LANES), _bf16),
        compiler_params=_cparams(("arbitrary", "arbitrary")),
    )(qm, km, vm, qg, kg, vg)


def _wo_router_kernel(x_ref, o_ref, mod_ref, wo_ref, g_ref, b_ref, wr_ref, br_ref,
                      x1_ref, h_ref, info_ref, infot_ref, cnt_ref, carry_ref, *, alpha):
    tm = x_ref.shape[1]
    first = (pl.program_id(0) == 0) & (pl.program_id(1) == 0)

    @pl.when(first)
    def _():
        carry_ref[...] = jnp.zeros_like(carry_ref)

    o = jnp.concatenate([o_ref[0, p] for p in range(o_ref.shape[1])], axis=-1)
    mix = jnp.dot(o, wo_ref[...], preferred_element_type=_f32)
    x1 = _layer_norm(alpha * x_ref[0] + mod_ref[2:3, :] * mix, g_ref[...], b_ref[...])
    x1_ref[0] = x1

    h = x1 * (1.0 + mod_ref[4:5, :]) + mod_ref[3:4, :]
    _store_row_tiles(h_ref, (0,), _pack_bf16_pairs(h))
    logits = jnp.dot(h.astype(_bf16), wr_ref[...], preferred_element_type=_f32) + br_ref[...]
    lane_i = lax.broadcasted_iota(jnp.int32, (tm, LANES), 1)
    lane = lane_i.astype(_f32)
    neg = jnp.float32(-jnp.inf)
    big = jnp.float32(LANES)

    def arg_first_max(vals):
        mx = jnp.max(vals, axis=-1, keepdims=True)
        idx = jnp.min(jnp.where(vals == mx, lane, big), axis=-1, keepdims=True)
        return mx, idx

    is_grp = (lane_i >= N_EXPERTS) & (lane_i < N_EXPERTS + N_GROUPS)
    lg = jnp.where(is_grp, logits, neg)
    gmax, gidx = arg_first_max(lg)
    p_star = 1.0 / jnp.sum(jnp.exp(lg - gmax), axis=-1, keepdims=True)
    g_star = gidx - N_EXPERTS
    in_grp = (lane_i < N_EXPERTS) & ((lane_i >> 3).astype(_f32) == g_star)
    le = jnp.where(in_grp, logits, neg)
    v1, e1 = arg_first_max(le)
    v2, e2 = arg_first_max(jnp.where(lane == e1, neg, le))
    t = jnp.exp(v2 - v1)
    w1 = p_star / (1.0 + t)
    w2 = p_star * t / (1.0 + t)

    onehot = ((lane == e1) | (lane == e2)).astype(_f32)
    r = lax.broadcasted_iota(jnp.int32, (tm, tm), 0)
    c = lax.broadcasted_iota(jnp.int32, (tm, tm), 1)
    tri = (c < r).astype(_bf16)
    prefix = jnp.dot(tri, onehot.astype(_bf16), preferred_element_type=_f32) + carry_ref[...]
    r1 = jnp.sum(jnp.where(lane == e1, prefix, 0.0), axis=-1, keepdims=True)
    r2 = jnp.sum(jnp.where(lane == e2, prefix, 0.0), axis=-1, keepdims=True)
    carry_ref[...] += jnp.sum(onehot, axis=0, keepdims=True)
    cnt_ref[...] = carry_ref[...]

    cols = [e1, e2, w1, w2, r1, r2]
    info = jnp.zeros((tm, LANES), _f32)
    for k, col in enumerate(cols):
        info = jnp.where(lane_i == k, col, info)
    info_ref[0] = info[:, 0:8]
    infot_ref[0, 0] = info.T[0:8, :]


def _wo_router(xa, o, mod, wo, g, b, wr, br, tm, n_ctx_tiles, alpha):
    B, T, D = xa.shape
    full = lambda a: pl.BlockSpec(a.shape, lambda bb, i: (0,) * a.ndim)
    row = pl.BlockSpec((1, tm, D), lambda bb, i: (bb, i, 0))
    return pl.pallas_call(
        functools.partial(_wo_router_kernel, alpha=alpha),
        grid=(B, T // tm),
        in_specs=[row,
                  pl.BlockSpec((1, 8, tm, LANES), lambda bb, i: (bb, 0, i, 0)),
                  pl.BlockSpec((None, None, 6, D), lambda bb, i: (bb, jnp.where(i >= n_ctx_tiles, 1, 0), 0, 0)),
                  full(wo), full(g), full(b), full(wr), full(br)],
        out_specs=[row, pl.BlockSpec((1, tm, D // 2 // LANES, LANES), lambda bb, i: (bb, i, 0, 0)),
                   pl.BlockSpec((1, tm, 8), lambda bb, i: (bb, i, 0)),
                   pl.BlockSpec((1, 1, 8, tm), lambda bb, i: (bb, i, 0, 0)),
                   pl.BlockSpec((1, LANES), lambda bb, i: (0, 0))],
        out_shape=[jax.ShapeDtypeStruct((B, T, D), _f32),
                   jax.ShapeDtypeStruct((B, T, D // 2 // LANES, LANES), jnp.uint32),
                   jax.ShapeDtypeStruct((B, T, 8), _f32),
                   jax.ShapeDtypeStruct((B, T // tm, 8, tm), _f32),
                   jax.ShapeDtypeStruct((1, LANES), _f32)],
        scratch_shapes=[pltpu.VMEM((1, LANES), _f32)],
        compiler_params=_cparams(("arbitrary", "arbitrary")),
    )(xa, o, mod, wo, g, b, wr, br)


def _row_copy(src_ref, row, dst_ref, slot, sem):
    return pltpu.make_async_copy(src_ref.at[row], dst_ref.at[slot], sem)


def _issue_rows(src_ref, idx_ref, dst_ref, n, sem):
    for r in range(n):
        _row_copy(src_ref, idx_ref[0, 0, r], dst_ref, r, sem).start()


def _wait_rows(src_ref, dst_ref, n, sem):
    for r in range(n):
        _row_copy(src_ref, 0, dst_ref, r, sem).wait()


def _expert_kernel(be_ref, nu_ref, tok0_ref, tok1_ref, tok2_ref, h_ref, wg_ref, wu_ref, wd_ref, y_ref,
                   xbuf0, xbuf1, xbuf2, wg_bf, wu_bf, wd_bf, sem):
    i = pl.program_id(0)
    rows = xbuf0.shape[0]
    n_used = nu_ref[0]
    active = i < n_used
    draining = (i == n_used) | (i == n_used + 1)
    phase = i % 3
    bufs = (xbuf0, xbuf1, xbuf2)

    @pl.when(i == 0)
    def _():
        _issue_rows(h_ref, tok0_ref, xbuf0, rows, sem.at[0])
        _issue_rows(h_ref, tok1_ref, xbuf1, rows, sem.at[1])

    new_expert = active & ((i == 0) | (be_ref[i] != be_ref[jnp.maximum(i - 1, 0)]))

    @pl.when(new_expert)
    def _():
        wg_bf[...] = wg_ref[0, 0].astype(_bf16)
        wu_bf[...] = wu_ref[0, 0].astype(_bf16)
        wd_bf[...] = wd_ref[0, 0].astype(_bf16)

    def step(p):
        q = (p + 2) % 3
        _wait_rows(h_ref, bufs[p], rows, sem.at[p])
        xb = _unpack_bf16_pairs(_load_row_tiles(bufs[p], 0, rows)).astype(_bf16)
        _issue_rows(h_ref, tok2_ref, bufs[q], rows, sem.at[q])
        g = jnp.dot(xb, wg_bf[...], preferred_element_type=_f32)
        u = jnp.dot(xb, wu_bf[...], preferred_element_type=_f32)
        a = (g * _sigmoid(g) * u).astype(_bf16)
        _store_row_tiles(y_ref, (), _pack_bf16_pairs(jnp.dot(a, wd_bf[...], preferred_element_type=_f32)))

    for p in range(3):
        pl.when(active & (phase == p))(functools.partial(step, p))
        pl.when(draining & (phase == p))(
            functools.partial(_wait_rows, h_ref, bufs[p], rows, sem.at[p]))

    @pl.when(jnp.logical_not(active))
    def _():
        y_ref[...] = jnp.zeros_like(y_ref)


def _experts(h2d, slot_tok, block_e, n_used, w_gate, w_up, w_down, layer, rows):
    n_blocks = slot_tok.shape[0]
    _, _, D, F = w_gate.shape
    S4 = D // 2 // LANES
    tok = lambda k: pl.BlockSpec((1, 1, rows), lambda i, be, nu: (jnp.minimum(i + k, n_blocks - 1), 0, 0),
                                 memory_space=pltpu.SMEM)
    grid_spec = pltpu.PrefetchScalarGridSpec(
        num_scalar_prefetch=2,
        grid=(n_blocks,),
        in_specs=[tok(0), tok(1), tok(2),
                  pl.BlockSpec(memory_space=pl.ANY),
                  pl.BlockSpec((1, 1, D, F), lambda i, be, nu: (layer, be[i], 0, 0)),
                  pl.BlockSpec((1, 1, D, F), lambda i, be, nu: (layer, be[i], 0, 0)),
                  pl.BlockSpec((1, 1, F, D), lambda i, be, nu: (layer, be[i], 0, 0))],
        out_specs=pl.BlockSpec((rows, S4, LANES), lambda i, be, nu: (i, 0, 0)),
        scratch_shapes=[pltpu.VMEM((rows, S4, LANES), jnp.uint32), pltpu.VMEM((rows, S4, LANES), jnp.uint32),
                        pltpu.VMEM((rows, S4, LANES), jnp.uint32),
                        pltpu.VMEM((D, F), _bf16), pltpu.VMEM((D, F), _bf16), pltpu.VMEM((F, D), _bf16),
                        pltpu.SemaphoreType.DMA((3,))],
    )
    return pl.pallas_call(
        _expert_kernel,
        grid_spec=grid_spec,
        out_shape=jax.ShapeDtypeStruct((n_blocks * rows, S4, LANES), jnp.uint32),
        compiler_params=_cparams(("arbitrary",)),
    )(block_e, n_used, slot_tok, slot_tok, slot_tok, h2d, w_gate, w_up, w_down)


def _combine_kernel(dest0_ref, dest1_ref, dest2_ref, x_ref, mod_ref, info_ref, yb_ref, g_ref, b_ref, out_ref,
                    ybuf0, ybuf1, ybuf2, sem, *, alpha, n_tiles):
    tm = x_ref.shape[0]
    g_id = pl.program_id(0)
    active = g_id < n_tiles
    phase = g_id % 3
    bufs = (ybuf0, ybuf1, ybuf2)

    @pl.when(g_id == 0)
    def _():
        _issue_rows(yb_ref, dest0_ref, ybuf0, 2 * tm, sem.at[0])
        _issue_rows(yb_ref, dest1_ref, ybuf1, 2 * tm, sem.at[1])

    def step(p):
        q = (p + 2) % 3
        buf = bufs[p]
        _wait_rows(yb_ref, buf, 2 * tm, sem.at[p])
        _issue_rows(yb_ref, dest2_ref, bufs[q], 2 * tm, sem.at[q])
        info = info_ref[...]
        y = (info[:, 2:3] * _unpack_bf16_pairs(_load_row_tiles(buf, 0, tm))
             + info[:, 3:4] * _unpack_bf16_pairs(_load_row_tiles(buf, tm, 2 * tm)))
        out_ref[...] = _layer_norm(alpha * x_ref[...] + mod_ref[5:6, :] * y, g_ref[...], b_ref[...])

    for p in range(3):
        pl.when(active & (phase == p))(functools.partial(step, p))
        pl.when(jnp.logical_not(active) & (phase == p))(
            functools.partial(_wait_rows, yb_ref, bufs[p], 2 * tm, sem.at[p]))


def _combine_ln(x1, mod, info, dest, yb, g, b, tm, n_ctx_tiles, alpha, skip_ctx):
    B, T, D = x1.shape
    nt = T // tm
    off = n_ctx_tiles if skip_ctx else 0
    ntl = nt - off
    n_tiles = B * ntl
    clamp = lambda s: jnp.minimum(s, n_tiles - 1)
    tile = lambda s: (s // ntl) * nt + off + s % ntl
    full = lambda a: pl.BlockSpec(a.shape, lambda s: (0,) * a.ndim)
    dspec = lambda k: pl.BlockSpec((1, 1, 2 * tm), lambda s: (tile(clamp(s + k)), 0, 0), memory_space=pltpu.SMEM)
    out = pl.pallas_call(
        functools.partial(_combine_kernel, alpha=alpha, n_tiles=n_tiles),
        grid=(n_tiles + 2,),
        in_specs=[dspec(0), dspec(1), dspec(2),
                  pl.BlockSpec((tm, D), lambda s: (tile(clamp(s)), 0)),
                  pl.BlockSpec((None, None, 6, D),
                               lambda s: (clamp(s) // ntl, jnp.where(clamp(s) % ntl + off >= n_ctx_tiles, 1, 0), 0, 0)),
                  pl.BlockSpec((tm, 8), lambda s: (tile(clamp(s)), 0)),
                  pl.BlockSpec(memory_space=pl.ANY),
                  full(g), full(b)],
        out_specs=pl.BlockSpec((tm, D), lambda s: (clamp(s), 0)),
        out_shape=jax.ShapeDtypeStruct((n_tiles * tm, D), _f32),
        scratch_shapes=[pltpu.VMEM((2 * tm, D // 2 // LANES, LANES), jnp.uint32)] * 3
                       + [pltpu.SemaphoreType.DMA((3,))],
        compiler_params=_cparams(("arbitrary",)),
    )(dest, dest, dest, x1.reshape(B * T, D), mod, info.reshape(B * T, 8), yb, g, b)
    return out.reshape(B, ntl * tm, D)


def _prep_weights(w_in, w_q_b, w_kv_b, w_o, w_router_grp, b_router_grp, w_router_exp, b_router_exp):
    L, D, _ = w_in.shape
    zeros = lambda *s: jnp.zeros((L,) + s, _f32)
    kr = jnp.concatenate([zeros(D, 64), w_in[:, :, 640:672], zeros(D, 32)], axis=-1)
    gq = w_in[:, :, 672:1184].reshape(L, D, GQA_HEADS, HEAD_DIM)
    gq = jnp.stack([gq[:, :, :4], gq[:, :, 4:]], axis=3).reshape(L, D, 512)
    win = jnp.concatenate([w_in[:, :, 0:640], kr, gq, w_in[:, :, 1184:1440]], axis=-1).astype(_bf16)

    qb = w_q_b.reshape(L, MLA_Q_LORA, MLA_HEADS, MLA_QK)
    wqb = jnp.concatenate([qb, zeros(MLA_Q_LORA, MLA_HEADS, LANES - MLA_QK)], axis=-1)
    wqb = wqb.reshape(L, MLA_Q_LORA, MLA_HEADS * LANES).astype(_bf16)

    kvb = w_kv_b.reshape(L, MLA_KV_LORA, MLA_HEADS, MLA_NOPE + MLA_V)
    kpart = jnp.concatenate([kvb[..., :MLA_NOPE], zeros(MLA_KV_LORA, MLA_HEADS, LANES - MLA_NOPE)], axis=-1)
    vpart = kvb[..., MLA_NOPE:].reshape(L, MLA_KV_LORA, MLA_HEADS * MLA_V)
    wkvb = jnp.concatenate([kpart.reshape(L, MLA_KV_LORA, MLA_HEADS * LANES), vpart], axis=-1).astype(_bf16)

    n_mla = MLA_HEADS * MLA_V
    og = w_o[:, n_mla:, :].reshape(L, GQA_HEADS, HEAD_DIM, D)
    og = jnp.stack([og[:, :4], og[:, 4:]], axis=2).reshape(L, GQA_HEADS * HEAD_DIM, D)
    wo = jnp.concatenate([w_o[:, :n_mla, :], og], axis=1).astype(_bf16)

    pad = LANES - N_EXPERTS - N_GROUPS
    wr = jnp.concatenate([w_router_exp, w_router_grp, zeros(D, pad)], axis=-1).astype(_bf16)
    br = jnp.concatenate([b_router_exp, b_router_grp, jnp.zeros((L, pad), _f32)], axis=-1).reshape(L, 1, LANES)
    return win, wqb, wkvb, wo, wr, br


def _rope_tables(S, C, g_gqa_q, g_gqa_k):
    pos = jnp.arange(S)
    row = (pos // GRID_W).astype(_f32)
    col = (pos % GRID_W).astype(_f32)

    def angles(half):
        inv = ROPE_BASE ** (-(jnp.arange(0, half, 2, dtype=_f32) / half))
        ar, ac = row[:, None] * inv, col[:, None] * inv
        return jnp.concatenate([ar, ar, ac, ac], axis=-1)

    def cos_sin(half):
        ang = angles(half)
        sign = jnp.tile(jnp.concatenate([-jnp.ones(half // 2), jnp.ones(half // 2)]), 2).astype(_f32)
        cos = jnp.concatenate([jnp.ones((C, 2 * half), _f32), jnp.cos(ang)], axis=0)
        sin = jnp.concatenate([jnp.zeros((C, 2 * half), _f32), jnp.sin(ang) * sign], axis=0)
        return cos, sin

    T = S + C
    cm, sm = cos_sin(MLA_ROPE // 2)
    a_m = jnp.concatenate([jnp.ones((T, 64), _f32), cm, jnp.ones((T, 32), _f32)], axis=-1)
    b_m = jnp.concatenate([jnp.zeros((T, 64), _f32), sm, jnp.zeros((T, 32), _f32)], axis=-1)
    tabs_m = (a_m * (MLA_SCALE * LOG2E), b_m * (MLA_SCALE * LOG2E), a_m, b_m)

    cg, sg = cos_sin(HEAD_DIM // 2)
    swap = jnp.arange(HEAD_DIM) ^ (HEAD_DIM // 4)

    def gqa(g, scale):
        a = g[:, None, :] * cg[None] * scale
        b = g[:, swap][:, None, :] * sg[None] * scale
        return jnp.tile(a, (1, 1, 2)), jnp.tile(b, (1, 1, 2))

    aq, bq = gqa(g_gqa_q, GQA_SCALE * LOG2E)
    ak, bk = gqa(g_gqa_k, 1.0)
    return tabs_m, (aq, bq, ak, bk)


def kernel(x, c, ctx, c_ctx, w_ada, b_ada, w_in, g_q_a, w_q_b, g_kv_a, w_kv_b, g_gqa_q, g_gqa_k,
           w_o, ln1_g, ln1_b, w_router_grp, b_router_grp, w_router_exp, b_router_exp,
           w_gate, w_up, w_down, ln2_g, ln2_b):
    B, S, D = x.shape
    C = ctx.shape[1]
    L = w_in.shape[0]
    T = C + S
    alpha = (2.0 * L) ** 0.25
    tm = 256 if C % 256 == 0 else 128
    assert C % tm == 0 and S % tm == 0
    n_ctx_tiles = C // tm
    rows = MOE_ROWS

    R = -(-(B + 1) // 8) * 8
    cc = jnp.concatenate([c, c_ctx[None, :], jnp.zeros((R - B - 1, D), _f32)], axis=0)
    mods = _adaln(cc, w_ada, b_ada)
    lat = mods[:, :B].reshape(L, B, 1, 6, D)
    cx = jnp.broadcast_to(mods[:, B].reshape(L, 1, 1, 6, D), (L, B, 1, 6, D))
    mods = jnp.concatenate([cx, lat], axis=2)

    win, wqb, wkvb, wo, wr, br = _prep_weights(w_in, w_q_b, w_kv_b, w_o, w_router_grp, b_router_grp,
                                               w_router_exp, b_router_exp)
    tabs_m, tabs_g = _rope_tables(S, C, g_gqa_q, g_gqa_k)
    blk = jnp.arange(GQA_HEADS * HEAD_DIM) // HEAD_DIM
    hs = (blk[:, None] == blk[None, :]).astype(_bf16)

    n_tok = B * T
    n_tiles = n_tok // tm
    n_assign = n_tok * TOP_K
    n_blocks = -(-n_assign // rows) + N_EXPERTS + 2
    n_slots = n_blocks * rows
    blk_start = jnp.arange(n_blocks, dtype=jnp.int32) * rows
    tok_ids = jnp.arange(n_tok, dtype=jnp.int32).reshape(n_tiles, 1, tm)
    tok_ids = jnp.broadcast_to(tok_ids, (n_tiles, TOP_K, tm)).reshape(-1)
    pad_j = jnp.arange(rows - 1, dtype=jnp.int32)[None, :]

    xa = jnp.concatenate([ctx, x], axis=1)
    for l in range(L):
        mod = mods[l]
        qm, km, vm, qg, kg, vg = _inproj(
            xa, mod, win[l], wqb[l], wkvb[l], g_q_a[l][None, :], g_kv_a[l][None, :], hs,
            tabs_m, tuple(t[l] for t in tabs_g), tm, n_ctx_tiles)
        o = _attention(qm, km, vm, qg, kg, vg, tm, C)
        x1, h, info, info_t, counts = _wo_router(xa, o, mod, wo[l], ln1_g[l][None, :], ln1_b[l][None, :],
                                                 wr[l], br[l], tm, n_ctx_tiles, alpha)
        cnt = counts[0, :N_EXPERTS].astype(jnp.int32)
        padded = (cnt + rows - 1) // rows * rows
        pad_end = jnp.cumsum(padded)
        pad_start = pad_end - padded
        info_t = info_t.reshape(n_tiles, 8, tm)
        eid = info_t[:, 0:2, :].astype(jnp.int32)
        rank = info_t[:, 4:6, :].astype(jnp.int32)
        dest = pad_start[eid] + rank
        pad_key = (pad_start + cnt)[:, None] + pad_j
        pad_key = jnp.where(pad_j < (padded - cnt)[:, None], pad_key, n_slots)
        keys = jnp.concatenate([dest.reshape(-1), pad_key.reshape(-1)])
        vals = jnp.concatenate([tok_ids, jnp.zeros((N_EXPERTS * (rows - 1),), jnp.int32)])
        _, slot_tok = lax.sort_key_val(keys, vals)
        slot_tok = jnp.concatenate([slot_tok, jnp.zeros((n_slots - slot_tok.shape[0],), jnp.int32)])
        block_e = jnp.sum((blk_start[:, None] >= pad_end[None, :]).astype(jnp.int32), axis=1)
        block_e = jnp.minimum(block_e, N_EXPERTS - 1)
        n_used = (pad_end[-1:] // rows).astype(jnp.int32)

        yb = _experts(h.reshape(n_tok, D // 2 // LANES, LANES), slot_tok.reshape(n_blocks, 1, rows), block_e, n_used,
                      w_gate, w_up, w_down, l, rows)
        xa = _combine_ln(x1, mod, info, dest.reshape(n_tiles, 1, TOP_K * tm), yb, ln2_g[l][None, :], ln2_b[l][None, :],
                         tm, n_ctx_tiles, alpha, skip_ctx=(l == L - 1))
    return xa
```

```python
import functools

import jax
import jax.numpy as jnp
from jax import lax
from jax.experimental import pallas as pl
from jax.experimental.pallas import tpu as pltpu

GRID_W = 64
ROPE_BASE = 10000.0
RMS_EPS = 1e-6
LN_EPS = 1e-5

MLA_HEADS = 8
MLA_Q_LORA = 384
MLA_KV_LORA = 256
MLA_NOPE = 64
MLA_ROPE = 32
MLA_V = 64
MLA_QK = MLA_NOPE + MLA_ROPE
MLA_SCALE = MLA_QK ** -0.5

HEAD_DIM = 64
GQA_HEADS = 8
GQA_KV_HEADS = 2
GQA_SCALE = HEAD_DIM ** -0.5
LOG2E = 1.4426950408889634

N_GROUPS = 4
EXPERTS_PER_GROUP = 8
N_EXPERTS = N_GROUPS * EXPERTS_PER_GROUP
TOP_K = 2

LANES = 128
MOE_ROWS = 256
VMEM_LIMIT = 56 * 1024 * 1024

_QA0, _KVA0, _KR0, _GQ0, _GK0, _GV0, _INW = 0, 384, 640, 768, 1280, 1408, 1536

_f32 = jnp.float32
_bf16 = jnp.bfloat16


def _cparams(sem):
    return pltpu.CompilerParams(dimension_semantics=sem, vmem_limit_bytes=VMEM_LIMIT)


def _sigmoid(v):
    return 1.0 / (1.0 + jnp.exp(-v))


def _layer_norm(v, g, b):
    mu = jnp.mean(v, axis=-1, keepdims=True)
    d = v - mu
    var = jnp.mean(d * d, axis=-1, keepdims=True)
    return d * lax.rsqrt(var + LN_EPS) * g + b


def _pack_bf16_pairs(v):
    w = v.shape[-1] // 2
    hi = lax.bitcast_convert_type(v[:, :w].astype(_bf16).astype(_f32), jnp.uint32)
    lo = lax.bitcast_convert_type(v[:, w:].astype(_bf16).astype(_f32), jnp.uint32)
    return hi | (lo >> 16)


def _unpack_bf16_pairs(u):
    hi = lax.bitcast_convert_type(u & jnp.uint32(0xFFFF0000), _f32)
    lo = lax.bitcast_convert_type(u << 16, _f32)
    return jnp.concatenate([hi, lo], axis=-1)


def _rot_half(y, dist):
    lane = lax.broadcasted_iota(jnp.int32, y.shape, y.ndim - 1)
    n = y.shape[-1]
    fwd = pltpu.roll(y, n - dist, y.ndim - 1)
    bwd = pltpu.roll(y, dist, y.ndim - 1)
    return jnp.where((lane & dist) == 0, fwd, bwd)


def _adaln_kernel(c_ref, w_ref, b_ref, o_ref):
    cc = c_ref[...]
    s = (cc * _sigmoid(cc)).astype(_bf16)
    o_ref[0] = jnp.dot(s, w_ref[0].astype(_bf16), preferred_element_type=_f32) + b_ref[0]


def _adaln(cc, w_ada, b_ada):
    L, D, N = w_ada.shape
    R = cc.shape[0]
    tn = 1536 if N % 1536 == 0 else N
    return pl.pallas_call(
        _adaln_kernel,
        grid=(L, N // tn),
        in_specs=[pl.BlockSpec((R, D), lambda l, j: (0, 0)),
                  pl.BlockSpec((1, D, tn), lambda l, j: (l, 0, j)),
                  pl.BlockSpec((1, 1, tn), lambda l, j: (l, 0, j))],
        out_specs=pl.BlockSpec((1, R, tn), lambda l, j: (l, 0, j)),
        out_shape=jax.ShapeDtypeStruct((L, R, N), _f32),
        compiler_params=_cparams(("arbitrary", "arbitrary")),
    )(cc, w_ada, b_ada.reshape(L, 1, N))


def _inproj_kernel(x_ref, mod_ref, win_ref, wqb_ref, wkvb_ref, gqa_ref, gkva_ref, hs_ref,
                   aqm_ref, bqm_ref, akm_ref, bkm_ref, aqg_ref, bqg_ref, akg_ref, bkg_ref,
                   qm_ref, km_ref, vm_ref, qg_ref, kg_ref, vg_ref):
    x = x_ref[0]
    h = (x * (1.0 + mod_ref[1:2, :]) + mod_ref[0:1, :]).astype(_bf16)
    z = jnp.dot(h, win_ref[...], preferred_element_type=_f32)

    q_a = z[:, _QA0:_KVA0]
    qn = q_a * lax.rsqrt(jnp.mean(q_a * q_a, axis=-1, keepdims=True) + RMS_EPS) * gqa_ref[...]
    q = jnp.dot(qn.astype(_bf16), wqb_ref[...], preferred_element_type=_f32)
    kv_a = z[:, _KVA0:_KR0]
    kvn = kv_a * lax.rsqrt(jnp.mean(kv_a * kv_a, axis=-1, keepdims=True) + RMS_EPS) * gkva_ref[...]
    kv = jnp.dot(kvn.astype(_bf16), wkvb_ref[...], preferred_element_type=_f32)
    aq, bq = aqm_ref[...], bqm_ref[...]
    for hd in range(MLA_HEADS):
        qh = q[:, hd * LANES:(hd + 1) * LANES]
        qm_ref[0, hd] = (qh * aq + _rot_half(qh, 8) * bq).astype(_bf16)
    kr = z[:, _KR0:_GQ0]
    kr = kr * akm_ref[...] + _rot_half(kr, 8) * bkm_ref[...]
    for hd in range(MLA_HEADS):
        km_ref[0, hd] = (kv[:, hd * LANES:(hd + 1) * LANES] + kr).astype(_bf16)
    v0 = MLA_HEADS * LANES
    ones = jnp.ones((x.shape[0], LANES), _bf16)
    for p in range(MLA_HEADS // 2):
        vm_ref[0, p] = jnp.concatenate([kv[:, v0 + p * LANES:v0 + (p + 1) * LANES].astype(_bf16), ones], axis=-1)

    def head_rs(t, hs):
        sq = t * t
        hi = sq.astype(_bf16)
        lo = (sq - hi.astype(_f32)).astype(_bf16)
        ssum = (jnp.dot(hi, hs, preferred_element_type=_f32)
                + jnp.dot(lo, hs, preferred_element_type=_f32))
        return lax.rsqrt(ssum * (1.0 / HEAD_DIM) + RMS_EPS)

    gq = z[:, _GQ0:_GK0]
    yq = gq * head_rs(gq, hs_ref[...])
    aqg, bqg = aqg_ref[...], bqg_ref[...]
    for j in range(GQA_HEADS // 2):
        yj = yq[:, j * LANES:(j + 1) * LANES]
        qg_ref[0, j] = (yj * aqg + _rot_half(yj, 16) * bqg).astype(_bf16)
    gk = z[:, _GK0:_GV0]
    yk = gk * head_rs(gk, hs_ref[0:LANES, 0:LANES])
    kg_ref[0] = (yk * akg_ref[...] + _rot_half(yk, 16) * bkg_ref[...]).astype(_bf16)
    vg_ref[0] = jnp.concatenate([z[:, _GV0:_INW].astype(_bf16), ones], axis=-1)


def _inproj(xa, mod, win, wqb, wkvb, gqa, gkva, hs, tabs_m, tabs_g, tm, n_ctx_tiles):
    B, T, D = xa.shape
    nt = T // tm
    full = lambda a: pl.BlockSpec(a.shape, lambda b, i: (0,) * a.ndim)
    tab = pl.BlockSpec((tm, LANES), lambda b, i: (i, 0))
    out4 = lambda n, w=LANES: pl.BlockSpec((1, n, tm, w), lambda b, i: (b, 0, i, 0))
    out3 = lambda w=LANES: pl.BlockSpec((1, tm, w), lambda b, i: (b, i, 0))
    sds = lambda *s: jax.ShapeDtypeStruct(s, _bf16)
    return pl.pallas_call(
        _inproj_kernel,
        grid=(B, nt),
        in_specs=[pl.BlockSpec((1, tm, D), lambda b, i: (b, i, 0)),
                  pl.BlockSpec((None, None, 6, D), lambda b, i: (b, jnp.where(i >= n_ctx_tiles, 1, 0), 0, 0)),
                  full(win), full(wqb), full(wkvb), full(gqa), full(gkva), full(hs),
                  tab, tab, tab, tab, tab, tab, tab, tab],
        out_specs=[out4(8), out4(8), out4(4, 2 * LANES), out4(4), out3(), out3(2 * LANES)],
        out_shape=[sds(B, 8, T, LANES), sds(B, 8, T, LANES), sds(B, 4, T, 2 * LANES),
                   sds(B, 4, T, LANES), sds(B, T, LANES), sds(B, T, 2 * LANES)],
        compiler_params=_cparams(("arbitrary", "arbitrary")),
    )(xa, mod, win, wqb, wkvb, gqa, gkva, hs, *tabs_m, *tabs_g)


def _softmax_pv(q, k, v_ext):
    s = lax.dot_general(q, k, (((1,), (1,)), ((), ())), preferred_element_type=_f32)
    m = jnp.max(s, axis=-1, keepdims=True)
    p = jnp.exp2((s - m).astype(_bf16))
    pv = jnp.dot(p, v_ext, preferred_element_type=_f32)
    return pv[:, 0:LANES] * (1.0 / pv[:, LANES:LANES + 1])


def _attn_kernel(qm_ref, km_ref, vm_ref, qg_ref, kg_ref, vg_ref, o_ref, *, n_ctx, n_ctx_tiles):
    tq = qm_ref.shape[2]
    n_all = km_ref.shape[2]
    lane = lax.broadcasted_iota(jnp.int32, (tq, LANES), 1)
    low = lane < (LANES // 2)

    def run(klen):
        def mla_pair(p, carry):
            v = vm_ref[0, p, 0:klen, :]
            o0 = _softmax_pv(qm_ref[0, 2 * p], km_ref[0, 2 * p, 0:klen, :], v)
            o1 = _softmax_pv(qm_ref[0, 2 * p + 1], km_ref[0, 2 * p + 1, 0:klen, :], v)
            o_ref[0, p] = jnp.where(low, o0, o1).astype(_bf16)
            return carry

        lax.fori_loop(0, MLA_HEADS // 2, mla_pair, 0, unroll=True)

        def gqa_slab(j, carry):
            qs = qg_ref[0, j]
            k = kg_ref[0, 0:klen, :]
            v = vg_ref[0, 0:klen, :]
            o0 = _softmax_pv(jnp.where(low, qs, jnp.zeros_like(qs)), k, v)
            o1 = _softmax_pv(jnp.where(low, jnp.zeros_like(qs), qs), k, v)
            o_ref[0, MLA_HEADS // 2 + j] = jnp.where(low, o0, o1).astype(_bf16)
            return carry

        lax.fori_loop(0, GQA_HEADS // 2, gqa_slab, 0, unroll=True)

    i = pl.program_id(1)

    @pl.when(i < n_ctx_tiles)
    def _():
        run(n_ctx)

    @pl.when(i >= n_ctx_tiles)
    def _():
        run(n_all)


def _attention(qm, km, vm, qg, kg, vg, tq, n_ctx):
    B, _, T, _ = qm.shape
    qspec = lambda n: pl.BlockSpec((1, n, tq, LANES), lambda b, i: (b, 0, i, 0))
    kspec = lambda n, w=LANES: pl.BlockSpec((1, n, T, w), lambda b, i: (b, 0, 0, 0))
    k3 = lambda w=LANES: pl.BlockSpec((1, T, w), lambda b, i: (b, 0, 0))
    return pl.pallas_call(
        functools.partial(_attn_kernel, n_ctx=n_ctx, n_ctx_tiles=n_ctx // tq),
        grid=(B, T // tq),
        in_specs=[qspec(8), kspec(8), kspec(4, 2 * LANES), qspec(4), k3(), k3(2 * LANES)],
        out_specs=qspec(8),
        out_shape=jax.ShapeDtypeStruct((B, 8, T, LANES), _bf16),
        compiler_params=_cparams(("arbitrary", "arbitrary")),
    )(qm, km, vm, qg, kg, vg)


def _wo_router_kernel(x_ref, o_ref, mod_ref, wo_ref, g_ref, b_ref, wr_ref, br_ref,
                      x1_ref, h_ref, info_ref, infot_ref, cnt_ref, carry_ref, *, alpha):
    tm = x_ref.shape[1]
    first = (pl.program_id(0) == 0) & (pl.program_id(1) == 0)

    @pl.when(first)
    def _():
        carry_ref[...] = jnp.zeros_like(carry_ref)

    o = jnp.concatenate([o_ref[0, p] for p in range(o_ref.shape[1])], axis=-1)
    mix = jnp.dot(o, wo_ref[...], preferred_element_type=_f32)
    x1 = _layer_norm(alpha * x_ref[0] + mod_ref[2:3, :] * mix, g_ref[...], b_ref[...])
    x1_ref[0] = x1

    h = x1 * (1.0 + mod_ref[4:5, :]) + mod_ref[3:4, :]
    h_ref[0] = _pack_bf16_pairs(h)
    logits = jnp.dot(h.astype(_bf16), wr_ref[...], preferred_element_type=_f32) + br_ref[...]
    lane_i = lax.broadcasted_iota(jnp.int32, (tm, LANES), 1)
    lane = lane_i.astype(_f32)
    neg = jnp.float32(-jnp.inf)
    big = jnp.float32(LANES)

    def arg_first_max(vals):
        mx = jnp.max(vals, axis=-1, keepdims=True)
        idx = jnp.min(jnp.where(vals == mx, lane, big), axis=-1, keepdims=True)
        return mx, idx

    is_grp = (lane_i >= N_EXPERTS) & (lane_i < N_EXPERTS + N_GROUPS)
    lg = jnp.where(is_grp, logits, neg)
    gmax, gidx = arg_first_max(lg)
    p_star = 1.0 / jnp.sum(jnp.exp(lg - gmax), axis=-1, keepdims=True)
    g_star = gidx - N_EXPERTS
    in_grp = (lane_i < N_EXPERTS) & ((lane_i >> 3).astype(_f32) == g_star)
    le = jnp.where(in_grp, logits, neg)
    v1, e1 = arg_first_max(le)
    v2, e2 = arg_first_max(jnp.where(lane == e1, neg, le))
    t = jnp.exp(v2 - v1)
    w1 = p_star / (1.0 + t)
    w2 = p_star * t / (1.0 + t)

    onehot = ((lane == e1) | (lane == e2)).astype(_f32)
    r = lax.broadcasted_iota(jnp.int32, (tm, tm), 0)
    c = lax.broadcasted_iota(jnp.int32, (tm, tm), 1)
    tri = (c < r).astype(_bf16)
    prefix = jnp.dot(tri, onehot.astype(_bf16), preferred_element_type=_f32) + carry_ref[...]
    r1 = jnp.sum(jnp.where(lane == e1, prefix, 0.0), axis=-1, keepdims=True)
    r2 = jnp.sum(jnp.where(lane == e2, prefix, 0.0), axis=-1, keepdims=True)
    carry_ref[...] += jnp.sum(onehot, axis=0, keepdims=True)
    cnt_ref[...] = carry_ref[...]

    cols = [e1, e2, w1, w2, r1, r2]
    info = jnp.zeros((tm, LANES), _f32)
    for k, col in enumerate(cols):
        info = jnp.where(lane_i == k, col, info)
    info_ref[0] = info[:, 0:8]
    infot_ref[0, 0] = info.T[0:8, :]


def _wo_router(xa, o, mod, wo, g, b, wr, br, tm, n_ctx_tiles, alpha):
    B, T, D = xa.shape
    full = lambda a: pl.BlockSpec(a.shape, lambda bb, i: (0,) * a.ndim)
    row = pl.BlockSpec((1, tm, D), lambda bb, i: (bb, i, 0))
    return pl.pallas_call(
        functools.partial(_wo_router_kernel, alpha=alpha),
        grid=(B, T // tm),
        in_specs=[row,
                  pl.BlockSpec((1, 8, tm, LANES), lambda bb, i: (bb, 0, i, 0)),
                  pl.BlockSpec((None, None, 6, D), lambda bb, i: (bb, jnp.where(i >= n_ctx_tiles, 1, 0), 0, 0)),
                  full(wo), full(g), full(b), full(wr), full(br)],
        out_specs=[row, pl.BlockSpec((1, tm, D // 2), lambda bb, i: (bb, i, 0)),
                   pl.BlockSpec((1, tm, 8), lambda bb, i: (bb, i, 0)),
                   pl.BlockSpec((1, 1, 8, tm), lambda bb, i: (bb, i, 0, 0)),
                   pl.BlockSpec((1, LANES), lambda bb, i: (0, 0))],
        out_shape=[jax.ShapeDtypeStruct((B, T, D), _f32),
                   jax.ShapeDtypeStruct((B, T, D // 2), jnp.uint32),
                   jax.ShapeDtypeStruct((B, T, 8), _f32),
                   jax.ShapeDtypeStruct((B, T // tm, 8, tm), _f32),
                   jax.ShapeDtypeStruct((1, LANES), _f32)],
        scratch_shapes=[pltpu.VMEM((1, LANES), _f32)],
        compiler_params=_cparams(("arbitrary", "arbitrary")),
    )(xa, o, mod, wo, g, b, wr, br)


def _row_copy(src_ref, row, dst_ref, slot, sem):
    return pltpu.make_async_copy(src_ref.at[pl.ds(row, 1), :], dst_ref.at[pl.ds(slot, 1), :], sem)


def _issue_rows(src_ref, idx_ref, dst_ref, n, sem):
    for r in range(n):
        _row_copy(src_ref, idx_ref[0, 0, r], dst_ref, r, sem).start(priority=r % 2)


def _wait_rows(src_ref, dst_ref, n, sem):
    for r in range(n):
        _row_copy(src_ref, 0, dst_ref, r, sem).wait()


def _expert_kernel(be_ref, nu_ref, tok0_ref, tok1_ref, tok2_ref, h_ref, wg_ref, wu_ref, wd_ref, y_ref,
                   xbuf0, xbuf1, xbuf2, wg_bf, wu_bf, wd_bf, sem):
    i = pl.program_id(0)
    rows = xbuf0.shape[0]
    n_used = nu_ref[0]
    active = i < n_used
    draining = (i == n_used) | (i == n_used + 1)
    phase = i % 3
    bufs = (xbuf0, xbuf1, xbuf2)

    @pl.when(i == 0)
    def _():
        _issue_rows(h_ref, tok0_ref, xbuf0, rows, sem.at[0])
        _issue_rows(h_ref, tok1_ref, xbuf1, rows, sem.at[1])

    new_expert = active & ((i == 0) | (be_ref[i] != be_ref[jnp.maximum(i - 1, 0)]))

    @pl.when(new_expert)
    def _():
        wg_bf[...] = wg_ref[0, 0].astype(_bf16)
        wu_bf[...] = wu_ref[0, 0].astype(_bf16)
        wd_bf[...] = wd_ref[0, 0].astype(_bf16)

    def step(p):
        q = (p + 2) % 3
        _wait_rows(h_ref, bufs[p], rows, sem.at[p])
        xb = _unpack_bf16_pairs(bufs[p][...]).astype(_bf16)
        _issue_rows(h_ref, tok2_ref, bufs[q], rows, sem.at[q])
        g = jnp.dot(xb, wg_bf[...], preferred_element_type=_f32)
        u = jnp.dot(xb, wu_bf[...], preferred_element_type=_f32)
        a = (g * _sigmoid(g) * u).astype(_bf16)
        y_ref[...] = _pack_bf16_pairs(jnp.dot(a, wd_bf[...], preferred_element_type=_f32))

    for p in range(3):
        pl.when(active & (phase == p))(functools.partial(step, p))
        pl.when(draining & (phase == p))(
            functools.partial(_wait_rows, h_ref, bufs[p], rows, sem.at[p]))

    @pl.when(jnp.logical_not(active))
    def _():
        y_ref[...] = jnp.zeros_like(y_ref)


def _experts(h2d, slot_tok, block_e, n_used, w_gate, w_up, w_down, layer, rows):
    n_blocks = slot_tok.shape[0]
    _, _, D, F = w_gate.shape
    tok = lambda k: pl.BlockSpec((1, 1, rows), lambda i, be, nu: (jnp.minimum(i + k, n_blocks - 1), 0, 0),
                                 memory_space=pltpu.SMEM)
    grid_spec = pltpu.PrefetchScalarGridSpec(
        num_scalar_prefetch=2,
        grid=(n_blocks,),
        in_specs=[tok(0), tok(1), tok(2),
                  pl.BlockSpec(memory_space=pl.ANY),
                  pl.BlockSpec((1, 1, D, F), lambda i, be, nu: (layer, be[i], 0, 0)),
                  pl.BlockSpec((1, 1, D, F), lambda i, be, nu: (layer, be[i], 0, 0)),
                  pl.BlockSpec((1, 1, F, D), lambda i, be, nu: (layer, be[i], 0, 0))],
        out_specs=pl.BlockSpec((rows, D // 2), lambda i, be, nu: (i, 0)),
        scratch_shapes=[pltpu.VMEM((rows, D // 2), jnp.uint32), pltpu.VMEM((rows, D // 2), jnp.uint32),
                        pltpu.VMEM((rows, D // 2), jnp.uint32),
                        pltpu.VMEM((D, F), _bf16), pltpu.VMEM((D, F), _bf16), pltpu.VMEM((F, D), _bf16),
                        pltpu.SemaphoreType.DMA((3,))],
    )
    return pl.pallas_call(
        _expert_kernel,
        grid_spec=grid_spec,
        out_shape=jax.ShapeDtypeStruct((n_blocks * rows, D // 2), jnp.uint32),
        compiler_params=_cparams(("arbitrary",)),
    )(block_e, n_used, slot_tok, slot_tok, slot_tok, h2d, w_gate, w_up, w_down)


def _combine_kernel(dest0_ref, dest1_ref, dest2_ref, x_ref, mod_ref, info_ref, yb_ref, g_ref, b_ref, out_ref,
                    ybuf0, ybuf1, ybuf2, sem, *, alpha, n_tiles):
    tm = x_ref.shape[0]
    g_id = pl.program_id(0)
    active = g_id < n_tiles
    phase = g_id % 3
    bufs = (ybuf0, ybuf1, ybuf2)

    @pl.when(g_id == 0)
    def _():
        _issue_rows(yb_ref, dest0_ref, ybuf0, 2 * tm, sem.at[0])
        _issue_rows(yb_ref, dest1_ref, ybuf1, 2 * tm, sem.at[1])

    def step(p):
        q = (p + 2) % 3
        buf = bufs[p]
        _wait_rows(yb_ref, buf, 2 * tm, sem.at[p])
        _issue_rows(yb_ref, dest2_ref, bufs[q], 2 * tm, sem.at[q])
        info = info_ref[...]
        y = (info[:, 2:3] * _unpack_bf16_pairs(buf[0:tm, :])
             + info[:, 3:4] * _unpack_bf16_pairs(buf[tm:2 * tm, :]))
        out_ref[...] = _layer_norm(alpha * x_ref[...] + mod_ref[5:6, :] * y, g_ref[...], b_ref[...])

    for p in range(3):
        pl.when(active & (phase == p))(functools.partial(step, p))
        pl.when(jnp.logical_not(active) & (phase == p))(
            functools.partial(_wait_rows, yb_ref, bufs[p], 2 * tm, sem.at[p]))


def _combine_ln(x1, mod, info, dest, yb, g, b, tm, n_ctx_tiles, alpha, skip_ctx):
    B, T, D = x1.shape
    nt = T // tm
    off = n_ctx_tiles if skip_ctx else 0
    ntl = nt - off
    n_tiles = B * ntl
    clamp = lambda s: jnp.minimum(s, n_tiles - 1)
    tile = lambda s: (s // ntl) * nt + off + s % ntl
    full = lambda a: pl.BlockSpec(a.shape, lambda s: (0,) * a.ndim)
    dspec = lambda k: pl.BlockSpec((1, 1, 2 * tm), lambda s: (tile(clamp(s + k)), 0, 0), memory_space=pltpu.SMEM)
    out = pl.pallas_call(
        functools.partial(_combine_kernel, alpha=alpha, n_tiles=n_tiles),
        grid=(n_tiles + 2,),
        in_specs=[dspec(0), dspec(1), dspec(2),
                  pl.BlockSpec((tm, D), lambda s: (tile(clamp(s)), 0)),
                  pl.BlockSpec((None, None, 6, D),
                               lambda s: (clamp(s) // ntl, jnp.where(clamp(s) % ntl + off >= n_ctx_tiles, 1, 0), 0, 0)),
                  pl.BlockSpec((tm, 8), lambda s: (tile(clamp(s)), 0)),
                  pl.BlockSpec(memory_space=pl.ANY),
                  full(g), full(b)],
        out_specs=pl.BlockSpec((tm, D), lambda s: (clamp(s), 0)),
        out_shape=jax.ShapeDtypeStruct((n_tiles * tm, D), _f32),
        scratch_shapes=[pltpu.VMEM((2 * tm, D // 2), jnp.uint32)] * 3 + [pltpu.SemaphoreType.DMA((3,))],
        compiler_params=_cparams(("arbitrary",)),
    )(dest, dest, dest, x1.reshape(B * T, D), mod, info.reshape(B * T, 8), yb, g, b)
    return out.reshape(B, ntl * tm, D)


def _prep_weights(w_in, w_q_b, w_kv_b, w_o, w_router_grp, b_router_grp, w_router_exp, b_router_exp):
    L, D, _ = w_in.shape
    zeros = lambda *s: jnp.zeros((L,) + s, _f32)
    kr = jnp.concatenate([zeros(D, 64), w_in[:, :, 640:672], zeros(D, 32)], axis=-1)
    gq = w_in[:, :, 672:1184].reshape(L, D, GQA_HEADS, HEAD_DIM)
    gq = jnp.stack([gq[:, :, :4], gq[:, :, 4:]], axis=3).reshape(L, D, 512)
    win = jnp.concatenate([w_in[:, :, 0:640], kr, gq, w_in[:, :, 1184:1440]], axis=-1).astype(_bf16)

    qb = w_q_b.reshape(L, MLA_Q_LORA, MLA_HEADS, MLA_QK)
    wqb = jnp.concatenate([qb, zeros(MLA_Q_LORA, MLA_HEADS, LANES - MLA_QK)], axis=-1)
    wqb = wqb.reshape(L, MLA_Q_LORA, MLA_HEADS * LANES).astype(_bf16)

    kvb = w_kv_b.reshape(L, MLA_KV_LORA, MLA_HEADS, MLA_NOPE + MLA_V)
    kpart = jnp.concatenate([kvb[..., :MLA_NOPE], zeros(MLA_KV_LORA, MLA_HEADS, LANES - MLA_NOPE)], axis=-1)
    vpart = kvb[..., MLA_NOPE:].reshape(L, MLA_KV_LORA, MLA_HEADS * MLA_V)
    wkvb = jnp.concatenate([kpart.reshape(L, MLA_KV_LORA, MLA_HEADS * LANES), vpart], axis=-1).astype(_bf16)

    n_mla = MLA_HEADS * MLA_V
    og = w_o[:, n_mla:, :].reshape(L, GQA_HEADS, HEAD_DIM, D)
    og = jnp.stack([og[:, :4], og[:, 4:]], axis=2).reshape(L, GQA_HEADS * HEAD_DIM, D)
    wo = jnp.concatenate([w_o[:, :n_mla, :], og], axis=1).astype(_bf16)

    pad = LANES - N_EXPERTS - N_GROUPS
    wr = jnp.concatenate([w_router_exp, w_router_grp, zeros(D, pad)], axis=-1).astype(_bf16)
    br = jnp.concatenate([b_router_exp, b_router_grp, jnp.zeros((L, pad), _f32)], axis=-1).reshape(L, 1, LANES)
    return win, wqb, wkvb, wo, wr, br


def _rope_tables(S, C, g_gqa_q, g_gqa_k):
    pos = jnp.arange(S)
    row = (pos // GRID_W).astype(_f32)
    col = (pos % GRID_W).astype(_f32)

    def angles(half):
        inv = ROPE_BASE ** (-(jnp.arange(0, half, 2, dtype=_f32) / half))
        ar, ac = row[:, None] * inv, col[:, None] * inv
        return jnp.concatenate([ar, ar, ac, ac], axis=-1)

    def cos_sin(half):
        ang = angles(half)
        sign = jnp.tile(jnp.concatenate([-jnp.ones(half // 2), jnp.ones(half // 2)]), 2).astype(_f32)
        cos = jnp.concatenate([jnp.ones((C, 2 * half), _f32), jnp.cos(ang)], axis=0)
        sin = jnp.concatenate([jnp.zeros((C, 2 * half), _f32), jnp.sin(ang) * sign], axis=0)
        return cos, sin

    T = S + C
    cm, sm = cos_sin(MLA_ROPE // 2)
    a_m = jnp.concatenate([jnp.ones((T, 64), _f32), cm, jnp.ones((T, 32), _f32)], axis=-1)
    b_m = jnp.concatenate([jnp.zeros((T, 64), _f32), sm, jnp.zeros((T, 32), _f32)], axis=-1)
    tabs_m = (a_m * (MLA_SCALE * LOG2E), b_m * (MLA_SCALE * LOG2E), a_m, b_m)

    cg, sg = cos_sin(HEAD_DIM // 2)
    swap = jnp.arange(HEAD_DIM) ^ (HEAD_DIM // 4)

    def gqa(g, scale):
        a = g[:, None, :] * cg[None] * scale
        b = g[:, swap][:, None, :] * sg[None] * scale
        return jnp.tile(a, (1, 1, 2)), jnp.tile(b, (1, 1, 2))

    aq, bq = gqa(g_gqa_q, GQA_SCALE * LOG2E)
    ak, bk = gqa(g_gqa_k, 1.0)
    return tabs_m, (aq, bq, ak, bk)


def kernel(x, c, ctx, c_ctx, w_ada, b_ada, w_in, g_q_a, w_q_b, g_kv_a, w_kv_b, g_gqa_q, g_gqa_k,
           w_o, ln1_g, ln1_b, w_router_grp, b_router_grp, w_router_exp, b_router_exp,
           w_gate, w_up, w_down, ln2_g, ln2_b):
    B, S, D = x.shape
    C = ctx.shape[1]
    L = w_in.shape[0]
    T = C + S
    alpha = (2.0 * L) ** 0.25
    tm = 256 if C % 256 == 0 else 128
    assert C % tm == 0 and S % tm == 0
    n_ctx_tiles = C // tm
    rows = MOE_ROWS

    R = -(-(B + 1) // 8) * 8
    cc = jnp.concatenate([c, c_ctx[None, :], jnp.zeros((R - B - 1, D), _f32)], axis=0)
    mods = _adaln(cc, w_ada, b_ada)
    lat = mods[:, :B].reshape(L, B, 1, 6, D)
    cx = jnp.broadcast_to(mods[:, B].reshape(L, 1, 1, 6, D), (L, B, 1, 6, D))
    mods = jnp.concatenate([cx, lat], axis=2)

    win, wqb, wkvb, wo, wr, br = _prep_weights(w_in, w_q_b, w_kv_b, w_o, w_router_grp, b_router_grp,
                                               w_router_exp, b_router_exp)
    tabs_m, tabs_g = _rope_tables(S, C, g_gqa_q, g_gqa_k)
    blk = jnp.arange(GQA_HEADS * HEAD_DIM) // HEAD_DIM
    hs = (blk[:, None] == blk[None, :]).astype(_bf16)

    n_tok = B * T
    n_tiles = n_tok // tm
    n_assign = n_tok * TOP_K
    n_blocks = -(-n_assign // rows) + N_EXPERTS + 2
    n_slots = n_blocks * rows
    blk_start = jnp.arange(n_blocks, dtype=jnp.int32) * rows
    tok_ids = jnp.arange(n_tok, dtype=jnp.int32).reshape(n_tiles, 1, tm)
    tok_ids = jnp.broadcast_to(tok_ids, (n_tiles, TOP_K, tm)).reshape(-1)
    pad_j = jnp.arange(rows - 1, dtype=jnp.int32)[None, :]

    xa = jnp.concatenate([ctx, x], axis=1)
    for l in range(L):
        mod = mods[l]
        qm, km, vm, qg, kg, vg = _inproj(
            xa, mod, win[l], wqb[l], wkvb[l], g_q_a[l][None, :], g_kv_a[l][None, :], hs,
            tabs_m, tuple(t[l] for t in tabs_g), tm, n_ctx_tiles)
        o = _attention(qm, km, vm, qg, kg, vg, tm, C)
        x1, h, info, info_t, counts = _wo_router(xa, o, mod, wo[l], ln1_g[l][None, :], ln1_b[l][None, :],
                                                 wr[l], br[l], tm, n_ctx_tiles, alpha)
        cnt = counts[0, :N_EXPERTS].astype(jnp.int32)
        padded = (cnt + rows - 1) // rows * rows
        pad_end = jnp.cumsum(padded)
        pad_start = pad_end - padded
        info_t = info_t.reshape(n_tiles, 8, tm)
        eid = info_t[:, 0:2, :].astype(jnp.int32)
        rank = info_t[:, 4:6, :].astype(jnp.int32)
        dest = rank
        for e in range(N_EXPERTS):
            dest = dest + jnp.where(eid == e, pad_start[e], 0)
        pad_key = (pad_start + cnt)[:, None] + pad_j
        pad_key = jnp.where(pad_j < (padded - cnt)[:, None], pad_key, n_slots)
        keys = jnp.concatenate([dest.reshape(-1), pad_key.reshape(-1)])
        vals = jnp.concatenate([tok_ids, jnp.zeros((N_EXPERTS * (rows - 1),), jnp.int32)])
        _, slot_tok = lax.sort_key_val(keys, vals)
        slot_tok = jnp.concatenate([slot_tok, jnp.zeros((n_slots - slot_tok.shape[0],), jnp.int32)])
        block_e = jnp.sum((blk_start[:, None] >= pad_end[None, :]).astype(jnp.int32), axis=1)
        block_e = jnp.minimum(block_e, N_EXPERTS - 1)
        n_used = (pad_end[-1:] // rows).astype(jnp.int32)

        yb = _experts(h.reshape(n_tok, D // 2), slot_tok.reshape(n_blocks, 1, rows), block_e, n_used,
                      w_gate, w_up, w_down, l, rows)
        xa = _combine_ln(x1, mod, info, dest.reshape(n_tiles, 1, TOP_K * tm), yb, ln2_g[l][None, :], ln2_b[l][None, :],
                         tm, n_ctx_tiles, alpha, skip_ctx=(l == L - 1))
    return xa
```

```python
import functools

import jax
import jax.numpy as jnp
from jax import lax
from jax.experimental import pallas as pl
from jax.experimental.pallas import tpu as pltpu

GRID_W = 64
ROPE_BASE = 10000.0
RMS_EPS = 1e-6
LN_EPS = 1e-5

MLA_HEADS = 8
MLA_Q_LORA = 384
MLA_KV_LORA = 256
MLA_NOPE = 64
MLA_ROPE = 32
MLA_V = 64
MLA_QK = MLA_NOPE + MLA_ROPE
MLA_SCALE = MLA_QK ** -0.5

HEAD_DIM = 64
GQA_HEADS = 8
GQA_KV_HEADS = 2
GQA_SCALE = HEAD_DIM ** -0.5
LOG2E = 1.4426950408889634

N_GROUPS = 4
EXPERTS_PER_GROUP = 8
N_EXPERTS = N_GROUPS * EXPERTS_PER_GROUP
TOP_K = 2

LANES = 128
MOE_ROWS = 256
IDX_GROUP = 8
VMEM_LIMIT = 56 * 1024 * 1024

_QA0, _KVA0, _KR0, _GQ0, _GK0, _GV0, _INW = 0, 384, 640, 768, 1280, 1408, 1536

_f32 = jnp.float32
_bf16 = jnp.bfloat16


def _cparams(sem):
    return pltpu.CompilerParams(dimension_semantics=sem, vmem_limit_bytes=VMEM_LIMIT)


def _sigmoid(v):
    return 1.0 / (1.0 + jnp.exp(-v))


def _layer_norm(v, g, b):
    mu = jnp.mean(v, axis=-1, keepdims=True)
    d = v - mu
    var = jnp.mean(d * d, axis=-1, keepdims=True)
    return d * lax.rsqrt(var + LN_EPS) * g + b


def _pack_bf16_pairs(v):
    w = v.shape[-1] // 2
    hi = lax.bitcast_convert_type(v[:, :w].astype(_bf16).astype(_f32), jnp.uint32)
    lo = lax.bitcast_convert_type(v[:, w:].astype(_bf16).astype(_f32), jnp.uint32)
    return hi | (lo >> 16)


def _unpack_bf16_pairs(u):
    hi = lax.bitcast_convert_type(u & jnp.uint32(0xFFFF0000), _f32)
    lo = lax.bitcast_convert_type(u << 16, _f32)
    return jnp.concatenate([hi, lo], axis=-1)


def _rot_half(y, dist):
    lane = lax.broadcasted_iota(jnp.int32, y.shape, y.ndim - 1)
    n = y.shape[-1]
    fwd = pltpu.roll(y, n - dist, y.ndim - 1)
    bwd = pltpu.roll(y, dist, y.ndim - 1)
    return jnp.where((lane & dist) == 0, fwd, bwd)


def _adaln_kernel(c_ref, w_ref, b_ref, o_ref):
    cc = c_ref[...]
    s = (cc * _sigmoid(cc)).astype(_bf16)
    o_ref[0] = jnp.dot(s, w_ref[0].astype(_bf16), preferred_element_type=_f32) + b_ref[0]


def _adaln(cc, w_ada, b_ada):
    L, D, N = w_ada.shape
    R = cc.shape[0]
    tn = 1536 if N % 1536 == 0 else N
    return pl.pallas_call(
        _adaln_kernel,
        grid=(L, N // tn),
        in_specs=[pl.BlockSpec((R, D), lambda l, j: (0, 0)),
                  pl.BlockSpec((1, D, tn), lambda l, j: (l, 0, j)),
                  pl.BlockSpec((1, 1, tn), lambda l, j: (l, 0, j))],
        out_specs=pl.BlockSpec((1, R, tn), lambda l, j: (l, 0, j)),
        out_shape=jax.ShapeDtypeStruct((L, R, N), _f32),
        compiler_params=_cparams(("arbitrary", "arbitrary")),
    )(cc, w_ada, b_ada.reshape(L, 1, N))


def _inproj_kernel(x_ref, mod_ref, win_ref, wqb_ref, wkvb_ref, gqa_ref, gkva_ref, hs_ref,
                   aqm_ref, bqm_ref, akm_ref, bkm_ref, aqg_ref, bqg_ref, akg_ref, bkg_ref,
                   qm_ref, km_ref, vm_ref, qg_ref, kg_ref, vg_ref):
    x = x_ref[0]
    h = (x * (1.0 + mod_ref[1:2, :]) + mod_ref[0:1, :]).astype(_bf16)
    z = jnp.dot(h, win_ref[...], preferred_element_type=_f32)

    q_a = z[:, _QA0:_KVA0]
    qn = q_a * lax.rsqrt(jnp.mean(q_a * q_a, axis=-1, keepdims=True) + RMS_EPS) * gqa_ref[...]
    q = jnp.dot(qn.astype(_bf16), wqb_ref[...], preferred_element_type=_f32)
    kv_a = z[:, _KVA0:_KR0]
    kvn = kv_a * lax.rsqrt(jnp.mean(kv_a * kv_a, axis=-1, keepdims=True) + RMS_EPS) * gkva_ref[...]
    kv = jnp.dot(kvn.astype(_bf16), wkvb_ref[...], preferred_element_type=_f32)
    aq, bq = aqm_ref[...], bqm_ref[...]
    for hd in range(MLA_HEADS):
        qh = q[:, hd * LANES:(hd + 1) * LANES]
        qm_ref[0, hd] = (qh * aq + _rot_half(qh, 8) * bq).astype(_bf16)
    kr = z[:, _KR0:_GQ0]
    kr = kr * akm_ref[...] + _rot_half(kr, 8) * bkm_ref[...]
    for hd in range(MLA_HEADS):
        km_ref[0, hd] = (kv[:, hd * LANES:(hd + 1) * LANES] + kr).astype(_bf16)
    v0 = MLA_HEADS * LANES
    ones = jnp.ones((x.shape[0], LANES), _bf16)
    for p in range(MLA_HEADS // 2):
        vm_ref[0, p] = jnp.concatenate([kv[:, v0 + p * LANES:v0 + (p + 1) * LANES].astype(_bf16), ones], axis=-1)

    def head_rs(t, hs):
        ssum = jnp.dot((t * t).astype(_bf16), hs, preferred_element_type=_f32)
        return lax.rsqrt(ssum * (1.0 / HEAD_DIM) + RMS_EPS)

    gq = z[:, _GQ0:_GK0]
    yq = gq * head_rs(gq, hs_ref[...])
    aqg, bqg = aqg_ref[...], bqg_ref[...]
    for j in range(GQA_HEADS // 2):
        yj = yq[:, j * LANES:(j + 1) * LANES]
        qg_ref[0, j] = (yj * aqg + _rot_half(yj, 16) * bqg).astype(_bf16)
    gk = z[:, _GK0:_GV0]
    yk = gk * head_rs(gk, hs_ref[0:LANES, 0:LANES])
    kg_ref[0] = (yk * akg_ref[...] + _rot_half(yk, 16) * bkg_ref[...]).astype(_bf16)
    vg_ref[0] = jnp.concatenate([z[:, _GV0:_INW].astype(_bf16), ones], axis=-1)


def _inproj(xa, mod, win, wqb, wkvb, gqa, gkva, hs, tabs_m, tabs_g, tm, n_ctx_tiles):
    B, T, D = xa.shape
    nt = T // tm
    full = lambda a: pl.BlockSpec(a.shape, lambda b, i: (0,) * a.ndim)
    tab = pl.BlockSpec((tm, LANES), lambda b, i: (i, 0))
    out4 = lambda n, w=LANES: pl.BlockSpec((1, n, tm, w), lambda b, i: (b, 0, i, 0))
    out3 = lambda w=LANES: pl.BlockSpec((1, tm, w), lambda b, i: (b, i, 0))
    sds = lambda *s: jax.ShapeDtypeStruct(s, _bf16)
    return pl.pallas_call(
        _inproj_kernel,
        grid=(B, nt),
        in_specs=[pl.BlockSpec((1, tm, D), lambda b, i: (b, i, 0)),
                  pl.BlockSpec((None, None, 6, D), lambda b, i: (b, jnp.where(i >= n_ctx_tiles, 1, 0), 0, 0)),
                  full(win), full(wqb), full(wkvb), full(gqa), full(gkva), full(hs),
                  tab, tab, tab, tab, tab, tab, tab, tab],
        out_specs=[out4(8), out4(8), out4(4, 2 * LANES), out4(4), out3(), out3(2 * LANES)],
        out_shape=[sds(B, 8, T, LANES), sds(B, 8, T, LANES), sds(B, 4, T, 2 * LANES),
                   sds(B, 4, T, LANES), sds(B, T, LANES), sds(B, T, 2 * LANES)],
        compiler_params=_cparams(("arbitrary", "arbitrary")),
    )(xa, mod, win, wqb, wkvb, gqa, gkva, hs, *tabs_m, *tabs_g)


def _softmax_pv(q, k, v_ext, wide):
    s = lax.dot_general(q, k, (((1,), (1,)), ((), ())), preferred_element_type=_f32)
    m = jnp.max(s, axis=-1, keepdims=True)
    if wide:
        p = jnp.exp2((s - m).astype(_bf16))
        pv = jnp.dot(p, v_ext, preferred_element_type=_f32)
        return pv[:, 0:LANES] * (1.0 / pv[:, LANES:LANES + 1])
    e = jnp.exp2(s - m)
    l = jnp.sum(e, axis=-1, keepdims=True)
    return jnp.dot(e.astype(_bf16), v_ext[:, 0:LANES], preferred_element_type=_f32) * (1.0 / l)


def _attn_kernel(qm_ref, km_ref, vm_ref, qg_ref, kg_ref, vg_ref, o_ref, *, n_ctx, n_ctx_tiles):
    tq = qm_ref.shape[2]
    n_all = km_ref.shape[2]
    lane = lax.broadcasted_iota(jnp.int32, (tq, LANES), 1)
    low = lane < (LANES // 2)

    def run(klen, wide):
        def mla_pair(p, carry):
            v = vm_ref[0, p, 0:klen, :]
            o0 = _softmax_pv(qm_ref[0, 2 * p], km_ref[0, 2 * p, 0:klen, :], v, wide)
            o1 = _softmax_pv(qm_ref[0, 2 * p + 1], km_ref[0, 2 * p + 1, 0:klen, :], v, wide)
            o_ref[0, p] = jnp.where(low, o0, o1).astype(_bf16)
            return carry

        lax.fori_loop(0, MLA_HEADS // 2, mla_pair, 0, unroll=True)

        def gqa_slab(j, carry):
            qs = qg_ref[0, j]
            k = kg_ref[0, 0:klen, :]
            v = vg_ref[0, 0:klen, :]
            o0 = _softmax_pv(jnp.where(low, qs, jnp.zeros_like(qs)), k, v, wide)
            o1 = _softmax_pv(jnp.where(low, jnp.zeros_like(qs), qs), k, v, wide)
            o_ref[0, MLA_HEADS // 2 + j] = jnp.where(low, o0, o1).astype(_bf16)
            return carry

        lax.fori_loop(0, GQA_HEADS // 2, gqa_slab, 0, unroll=True)

    i = pl.program_id(1)

    @pl.when(i < n_ctx_tiles)
    def _():
        run(n_ctx, False)

    @pl.when(i >= n_ctx_tiles)
    def _():
        run(n_all, True)


def _attention(qm, km, vm, qg, kg, vg, tq, n_ctx):
    B, _, T, _ = qm.shape
    qspec = lambda n: pl.BlockSpec((1, n, tq, LANES), lambda b, i: (b, 0, i, 0))
    kspec = lambda n, w=LANES: pl.BlockSpec((1, n, T, w), lambda b, i: (b, 0, 0, 0))
    k3 = lambda w=LANES: pl.BlockSpec((1, T, w), lambda b, i: (b, 0, 0))
    return pl.pallas_call(
        functools.partial(_attn_kernel, n_ctx=n_ctx, n_ctx_tiles=n_ctx // tq),
        grid=(B, T // tq),
        in_specs=[qspec(8), kspec(8), kspec(4, 2 * LANES), qspec(4), k3(), k3(2 * LANES)],
        out_specs=qspec(8),
        out_shape=jax.ShapeDtypeStruct((B, 8, T, LANES), _bf16),
        compiler_params=_cparams(("arbitrary", "arbitrary")),
    )(qm, km, vm, qg, kg, vg)


def _wo_router_kernel(x_ref, o_ref, mod_ref, wo_ref, g_ref, b_ref, wr_ref, br_ref,
                      x1_ref, h_ref, info_ref, infot_ref, cnt_ref, carry_ref, *, alpha):
    tm = x_ref.shape[1]
    first = (pl.program_id(0) == 0) & (pl.program_id(1) == 0)

    @pl.when(first)
    def _():
        carry_ref[...] = jnp.zeros_like(carry_ref)

    o = jnp.concatenate([o_ref[0, p] for p in range(o_ref.shape[1])], axis=-1)
    mix = jnp.dot(o, wo_ref[...], preferred_element_type=_f32)
    x1 = _layer_norm(alpha * x_ref[0] + mod_ref[2:3, :] * mix, g_ref[...], b_ref[...])
    x1_ref[0] = x1

    h = x1 * (1.0 + mod_ref[4:5, :]) + mod_ref[3:4, :]
    h_ref[0] = _pack_bf16_pairs(h)
    logits = jnp.dot(h.astype(_bf16), wr_ref[...], preferred_element_type=_f32) + br_ref[...]
    lane_i = lax.broadcasted_iota(jnp.int32, (tm, LANES), 1)
    lane = lane_i.astype(_f32)
    neg = jnp.float32(-jnp.inf)
    big = jnp.float32(LANES)

    def arg_first_max(vals):
        mx = jnp.max(vals, axis=-1, keepdims=True)
        idx = jnp.min(jnp.where(vals == mx, lane, big), axis=-1, keepdims=True)
        return mx, idx

    is_grp = (lane_i >= N_EXPERTS) & (lane_i < N_EXPERTS + N_GROUPS)
    lg = jnp.where(is_grp, logits, neg)
    gmax, gidx = arg_first_max(lg)
    p_star = 1.0 / jnp.sum(jnp.exp(lg - gmax), axis=-1, keepdims=True)
    g_star = gidx - N_EXPERTS
    in_grp = (lane_i < N_EXPERTS) & ((lane_i >> 3).astype(_f32) == g_star)
    le = jnp.where(in_grp, logits, neg)
    v1, e1 = arg_first_max(le)
    v2, e2 = arg_first_max(jnp.where(lane == e1, neg, le))
    t = jnp.exp(v2 - v1)
    w1 = p_star / (1.0 + t)
    w2 = p_star * t / (1.0 + t)

    onehot = ((lane == e1) | (lane == e2)).astype(_f32)
    r = lax.broadcasted_iota(jnp.int32, (tm, tm), 0)
    c = lax.broadcasted_iota(jnp.int32, (tm, tm), 1)
    tri = (c < r).astype(_bf16)
    prefix = jnp.dot(tri, onehot.astype(_bf16), preferred_element_type=_f32) + carry_ref[...]
    r1 = jnp.sum(jnp.where(lane == e1, prefix, 0.0), axis=-1, keepdims=True)
    r2 = jnp.sum(jnp.where(lane == e2, prefix, 0.0), axis=-1, keepdims=True)
    carry_ref[...] += jnp.sum(onehot, axis=0, keepdims=True)
    cnt_ref[...] = carry_ref[...]

    cols = [e1, e2, w1, w2, r1, r2]
    info = jnp.zeros((tm, LANES), _f32)
    for k, col in enumerate(cols):
        info = jnp.where(lane_i == k, col, info)
    info_ref[0] = info[:, 0:8]
    infot_ref[0, 0] = info.T[0:8, :]


def _wo_router(xa, o, mod, wo, g, b, wr, br, tm, n_ctx_tiles, alpha):
    B, T, D = xa.shape
    full = lambda a: pl.BlockSpec(a.shape, lambda bb, i: (0,) * a.ndim)
    row = pl.BlockSpec((1, tm, D), lambda bb, i: (bb, i, 0))
    return pl.pallas_call(
        functools.partial(_wo_router_kernel, alpha=alpha),
        grid=(B, T // tm),
        in_specs=[row,
                  pl.BlockSpec((1, 8, tm, LANES), lambda bb, i: (bb, 0, i, 0)),
                  pl.BlockSpec((None, None, 6, D), lambda bb, i: (bb, jnp.where(i >= n_ctx_tiles, 1, 0), 0, 0)),
                  full(wo), full(g), full(b), full(wr), full(br)],
        out_specs=[row, pl.BlockSpec((1, tm, D // 2), lambda bb, i: (bb, i, 0)),
                   pl.BlockSpec((1, tm, 8), lambda bb, i: (bb, i, 0)),
                   pl.BlockSpec((1, 1, 8, tm), lambda bb, i: (bb, i, 0, 0)),
                   pl.BlockSpec((1, LANES), lambda bb, i: (0, 0))],
        out_shape=[jax.ShapeDtypeStruct((B, T, D), _f32),
                   jax.ShapeDtypeStruct((B, T, D // 2), jnp.uint32),
                   jax.ShapeDtypeStruct((B, T, 8), _f32),
                   jax.ShapeDtypeStruct((B, T // tm, 8, tm), _f32),
                   jax.ShapeDtypeStruct((1, LANES), _f32)],
        scratch_shapes=[pltpu.VMEM((1, LANES), _f32)],
        compiler_params=_cparams(("arbitrary", "arbitrary")),
    )(xa, o, mod, wo, g, b, wr, br)


def _row_copy(src_ref, row, dst_ref, slot, sem):
    return pltpu.make_async_copy(src_ref.at[pl.ds(row, 1), :], dst_ref.at[pl.ds(slot, 1), :], sem)


def _issue_rows(src_ref, idx_ref, j, dst_ref, n, sem):
    for r in range(n):
        _row_copy(src_ref, idx_ref[0, j, r], dst_ref, r, sem).start(priority=r % 2)


def _wait_rows(src_ref, dst_ref, n, sem):
    for r in range(n):
        _row_copy(src_ref, 0, dst_ref, r, sem).wait()


def _expert_kernel(be_ref, nu_ref, tok_ref, h_ref, wg_ref, wu_ref, wd_ref, y_ref,
                   xbuf0, xbuf1, xbuf2, wg_bf, wu_bf, wd_bf, sem):
    i = pl.program_id(0)
    rows = xbuf0.shape[0]
    n_used = nu_ref[0]
    active = i < n_used
    draining = (i == n_used) | (i == n_used + 1)
    phase = i % 3
    bufs = (xbuf0, xbuf1, xbuf2)
    ahead = jnp.minimum(i + 2, pl.num_programs(0) - 1) % IDX_GROUP

    @pl.when(i == 0)
    def _():
        _issue_rows(h_ref, tok_ref, 0, xbuf0, rows, sem.at[0])
        _issue_rows(h_ref, tok_ref, 1, xbuf1, rows, sem.at[1])

    new_expert = active & ((i == 0) | (be_ref[i] != be_ref[jnp.maximum(i - 1, 0)]))

    @pl.when(new_expert)
    def _():
        wg_bf[...] = wg_ref[0, 0].astype(_bf16)
        wu_bf[...] = wu_ref[0, 0].astype(_bf16)
        wd_bf[...] = wd_ref[0, 0].astype(_bf16)

    def step(p):
        q = (p + 2) % 3
        _wait_rows(h_ref, bufs[p], rows, sem.at[p])
        xb = _unpack_bf16_pairs(bufs[p][...]).astype(_bf16)
        _issue_rows(h_ref, tok_ref, ahead, bufs[q], rows, sem.at[q])
        g = jnp.dot(xb, wg_bf[...], preferred_element_type=_f32)
        u = jnp.dot(xb, wu_bf[...], preferred_element_type=_f32)
        a = (g * _sigmoid(g) * u).astype(_bf16)
        y_ref[...] = _pack_bf16_pairs(jnp.dot(a, wd_bf[...], preferred_element_type=_f32))

    for p in range(3):
        pl.when(active & (phase == p))(functools.partial(step, p))
        pl.when(draining & (phase == p))(
            functools.partial(_wait_rows, h_ref, bufs[p], rows, sem.at[p]))

    @pl.when(jnp.logical_not(active))
    def _():
        y_ref[...] = jnp.zeros_like(y_ref)


def _experts(h2d, slot_tok, block_e, n_used, w_gate, w_up, w_down, layer, rows):
    _, _, D, F = w_gate.shape
    n_blocks = slot_tok.shape[0] * IDX_GROUP
    tok = pl.BlockSpec((1, IDX_GROUP, rows), lambda i, be, nu: (jnp.minimum(i + 2, n_blocks - 1) // IDX_GROUP, 0, 0),
                       memory_space=pltpu.SMEM)
    grid_spec = pltpu.PrefetchScalarGridSpec(
        num_scalar_prefetch=2,
        grid=(n_blocks,),
        in_specs=[tok,
                  pl.BlockSpec(memory_space=pl.ANY),
                  pl.BlockSpec((1, 1, D, F), lambda i, be, nu: (layer, be[i], 0, 0)),
                  pl.BlockSpec((1, 1, D, F), lambda i, be, nu: (layer, be[i], 0, 0)),
                  pl.BlockSpec((1, 1, F, D), lambda i, be, nu: (layer, be[i], 0, 0))],
        out_specs=pl.BlockSpec((rows, D // 2), lambda i, be, nu: (i, 0)),
        scratch_shapes=[pltpu.VMEM((rows, D // 2), jnp.uint32), pltpu.VMEM((rows, D // 2), jnp.uint32),
                        pltpu.VMEM((rows, D // 2), jnp.uint32),
                        pltpu.VMEM((D, F), _bf16), pltpu.VMEM((D, F), _bf16), pltpu.VMEM((F, D), _bf16),
                        pltpu.SemaphoreType.DMA((3,))],
    )
    return pl.pallas_call(
        _expert_kernel,
        grid_spec=grid_spec,
        out_shape=jax.ShapeDtypeStruct((n_blocks * rows, D // 2), jnp.uint32),
        compiler_params=_cparams(("arbitrary",)),
    )(block_e, n_used, slot_tok, h2d, w_gate, w_up, w_down)


def _combine_kernel(dest_ref, x_ref, mod_ref, info_ref, yb_ref, g_ref, b_ref, out_ref,
                    ybuf0, ybuf1, ybuf2, sem, *, alpha, n_tiles):
    tm = x_ref.shape[0]
    g_id = pl.program_id(0)
    active = g_id < n_tiles
    phase = g_id % 3
    bufs = (ybuf0, ybuf1, ybuf2)
    ahead = jnp.minimum(g_id + 2, n_tiles - 1) % IDX_GROUP

    @pl.when(g_id == 0)
    def _():
        _issue_rows(yb_ref, dest_ref, 0, ybuf0, 2 * tm, sem.at[0])
        _issue_rows(yb_ref, dest_ref, min(1, n_tiles - 1), ybuf1, 2 * tm, sem.at[1])

    def step(p):
        q = (p + 2) % 3
        buf = bufs[p]
        _wait_rows(yb_ref, buf, 2 * tm, sem.at[p])
        _issue_rows(yb_ref, dest_ref, ahead, bufs[q], 2 * tm, sem.at[q])
        info = info_ref[...]
        y = (info[:, 2:3] * _unpack_bf16_pairs(buf[0:tm, :])
             + info[:, 3:4] * _unpack_bf16_pairs(buf[tm:2 * tm, :]))
        out_ref[...] = _layer_norm(alpha * x_ref[...] + mod_ref[5:6, :] * y, g_ref[...], b_ref[...])

    for p in range(3):
        pl.when(active & (phase == p))(functools.partial(step, p))
        pl.when(jnp.logical_not(active) & (phase == p))(
            functools.partial(_wait_rows, yb_ref, bufs[p], 2 * tm, sem.at[p]))


def _combine_ln(x1, mod, info, dest, yb, g, b, tm, n_ctx_tiles, alpha, skip_ctx):
    B, T, D = x1.shape
    nt = T // tm
    off = n_ctx_tiles if skip_ctx else 0
    ntl = nt - off
    n_tiles = B * ntl
    clamp = lambda s: jnp.minimum(s, n_tiles - 1)
    tile = lambda s: (s // ntl) * nt + off + s % ntl
    full = lambda a: pl.BlockSpec(a.shape, lambda s: (0,) * a.ndim)
    dest = dest.reshape(B, nt, 2 * tm)[:, off:].reshape(n_tiles, 2 * tm)
    n_grp = -(-n_tiles // IDX_GROUP)
    dest = jnp.concatenate([dest, jnp.zeros((n_grp * IDX_GROUP - n_tiles, 2 * tm), jnp.int32)])
    dest = dest.reshape(n_grp, IDX_GROUP, 2 * tm)
    dspec = pl.BlockSpec((1, IDX_GROUP, 2 * tm), lambda s: (clamp(s + 2) // IDX_GROUP, 0, 0),
                         memory_space=pltpu.SMEM)
    out = pl.pallas_call(
        functools.partial(_combine_kernel, alpha=alpha, n_tiles=n_tiles),
        grid=(n_tiles + 2,),
        in_specs=[dspec,
                  pl.BlockSpec((tm, D), lambda s: (tile(clamp(s)), 0)),
                  pl.BlockSpec((None, None, 6, D),
                               lambda s: (clamp(s) // ntl, jnp.where(clamp(s) % ntl + off >= n_ctx_tiles, 1, 0), 0, 0)),
                  pl.BlockSpec((tm, 8), lambda s: (tile(clamp(s)), 0)),
                  pl.BlockSpec(memory_space=pl.ANY),
                  full(g), full(b)],
        out_specs=pl.BlockSpec((tm, D), lambda s: (clamp(s), 0)),
        out_shape=jax.ShapeDtypeStruct((n_tiles * tm, D), _f32),
        scratch_shapes=[pltpu.VMEM((2 * tm, D // 2), jnp.uint32)] * 3 + [pltpu.SemaphoreType.DMA((3,))],
        compiler_params=_cparams(("arbitrary",)),
    )(dest, x1.reshape(B * T, D), mod, info.reshape(B * T, 8), yb, g, b)
    return out.reshape(B, ntl * tm, D)


def _prep_weights(w_in, w_q_b, w_kv_b, w_o, w_router_grp, b_router_grp, w_router_exp, b_router_exp):
    L, D, _ = w_in.shape
    zeros = lambda *s: jnp.zeros((L,) + s, _f32)
    kr = jnp.concatenate([zeros(D, 64), w_in[:, :, 640:672], zeros(D, 32)], axis=-1)
    gq = w_in[:, :, 672:1184].reshape(L, D, GQA_HEADS, HEAD_DIM)
    gq = jnp.stack([gq[:, :, :4], gq[:, :, 4:]], axis=3).reshape(L, D, 512)
    win = jnp.concatenate([w_in[:, :, 0:640], kr, gq, w_in[:, :, 1184:1440]], axis=-1).astype(_bf16)

    qb = w_q_b.reshape(L, MLA_Q_LORA, MLA_HEADS, MLA_QK)
    wqb = jnp.concatenate([qb, zeros(MLA_Q_LORA, MLA_HEADS, LANES - MLA_QK)], axis=-1)
    wqb = wqb.reshape(L, MLA_Q_LORA, MLA_HEADS * LANES).astype(_bf16)

    kvb = w_kv_b.reshape(L, MLA_KV_LORA, MLA_HEADS, MLA_NOPE + MLA_V)
    kpart = jnp.concatenate([kvb[..., :MLA_NOPE], zeros(MLA_KV_LORA, MLA_HEADS, LANES - MLA_NOPE)], axis=-1)
    vpart = kvb[..., MLA_NOPE:].reshape(L, MLA_KV_LORA, MLA_HEADS * MLA_V)
    wkvb = jnp.concatenate([kpart.reshape(L, MLA_KV_LORA, MLA_HEADS * LANES), vpart], axis=-1).astype(_bf16)

    n_mla = MLA_HEADS * MLA_V
    og = w_o[:, n_mla:, :].reshape(L, GQA_HEADS, HEAD_DIM, D)
    og = jnp.stack([og[:, :4], og[:, 4:]], axis=2).reshape(L, GQA_HEADS * HEAD_DIM, D)
    wo = jnp.concatenate([w_o[:, :n_mla, :], og], axis=1).astype(_bf16)

    pad = LANES - N_EXPERTS - N_GROUPS
    wr = jnp.concatenate([w_router_exp, w_router_grp, zeros(D, pad)], axis=-1).astype(_bf16)
    br = jnp.concatenate([b_router_exp, b_router_grp, jnp.zeros((L, pad), _f32)], axis=-1).reshape(L, 1, LANES)
    return win, wqb, wkvb, wo, wr, br


def _rope_tables(S, C, g_gqa_q, g_gqa_k):
    pos = jnp.arange(S)
    row = (pos // GRID_W).astype(_f32)
    col = (pos % GRID_W).astype(_f32)

    def angles(half):
        inv = ROPE_BASE ** (-(jnp.arange(0, half, 2, dtype=_f32) / half))
        ar, ac = row[:, None] * inv, col[:, None] * inv
        return jnp.concatenate([ar, ar, ac, ac], axis=-1)

    def cos_sin(half):
        ang = angles(half)
        sign = jnp.tile(jnp.concatenate([-jnp.ones(half // 2), jnp.ones(half // 2)]), 2).astype(_f32)
        cos = jnp.concatenate([jnp.ones((C, 2 * half), _f32), jnp.cos(ang)], axis=0)
        sin = jnp.concatenate([jnp.zeros((C, 2 * half), _f32), jnp.sin(ang) * sign], axis=0)
        return cos, sin

    T = S + C
    cm, sm = cos_sin(MLA_ROPE // 2)
    a_m = jnp.concatenate([jnp.ones((T, 64), _f32), cm, jnp.ones((T, 32), _f32)], axis=-1)
    b_m = jnp.concatenate([jnp.zeros((T, 64), _f32), sm, jnp.zeros((T, 32), _f32)], axis=-1)
    tabs_m = (a_m * (MLA_SCALE * LOG2E), b_m * (MLA_SCALE * LOG2E), a_m, b_m)

    cg, sg = cos_sin(HEAD_DIM // 2)
    swap = jnp.arange(HEAD_DIM) ^ (HEAD_DIM // 4)

    def gqa(g, scale):
        a = g[:, None, :] * cg[None] * scale
        b = g[:, swap][:, None, :] * sg[None] * scale
        return jnp.tile(a, (1, 1, 2)), jnp.tile(b, (1, 1, 2))

    aq, bq = gqa(g_gqa_q, GQA_SCALE * LOG2E)
    ak, bk = gqa(g_gqa_k, 1.0)
    return tabs_m, (aq, bq, ak, bk)


def kernel(x, c, ctx, c_ctx, w_ada, b_ada, w_in, g_q_a, w_q_b, g_kv_a, w_kv_b, g_gqa_q, g_gqa_k,
           w_o, ln1_g, ln1_b, w_router_grp, b_router_grp, w_router_exp, b_router_exp,
           w_gate, w_up, w_down, ln2_g, ln2_b):
    B, S, D = x.shape
    C = ctx.shape[1]
    L = w_in.shape[0]
    T = C + S
    alpha = (2.0 * L) ** 0.25
    tm = 256 if C % 256 == 0 else 128
    assert C % tm == 0 and S % tm == 0
    n_ctx_tiles = C // tm
    rows = MOE_ROWS

    R = -(-(B + 1) // 8) * 8
    cc = jnp.concatenate([c, c_ctx[None, :], jnp.zeros((R - B - 1, D), _f32)], axis=0)
    mods = _adaln(cc, w_ada, b_ada)
    lat = mods[:, :B].reshape(L, B, 1, 6, D)
    cx = jnp.broadcast_to(mods[:, B].reshape(L, 1, 1, 6, D), (L, B, 1, 6, D))
    mods = jnp.concatenate([cx, lat], axis=2)

    win, wqb, wkvb, wo, wr, br = _prep_weights(w_in, w_q_b, w_kv_b, w_o, w_router_grp, b_router_grp,
                                               w_router_exp, b_router_exp)
    tabs_m, tabs_g = _rope_tables(S, C, g_gqa_q, g_gqa_k)
    blk = jnp.arange(GQA_HEADS * HEAD_DIM) // HEAD_DIM
    hs = (blk[:, None] == blk[None, :]).astype(_bf16)

    n_tok = B * T
    n_tiles = n_tok // tm
    n_assign = n_tok * TOP_K
    n_blocks = -(-n_assign // rows) + N_EXPERTS + 2
    n_blocks = -(-n_blocks // IDX_GROUP) * IDX_GROUP
    n_slots = n_blocks * rows
    blk_start = jnp.arange(n_blocks, dtype=jnp.int32) * rows
    tok_ids = jnp.arange(n_tok, dtype=jnp.int32).reshape(n_tiles, 1, tm)
    tok_ids = jnp.broadcast_to(tok_ids, (n_tiles, TOP_K, tm)).reshape(-1)
    pad_j = jnp.arange(rows - 1, dtype=jnp.int32)[None, :]

    xa = jnp.concatenate([ctx, x], axis=1)
    for l in range(L):
        mod = mods[l]
        qm, km, vm, qg, kg, vg = _inproj(
            xa, mod, win[l], wqb[l], wkvb[l], g_q_a[l][None, :], g_kv_a[l][None, :], hs,
            tabs_m, tuple(t[l] for t in tabs_g), tm, n_ctx_tiles)
        o = _attention(qm, km, vm, qg, kg, vg, tm, C)
        x1, h, info, info_t, counts = _wo_router(xa, o, mod, wo[l], ln1_g[l][None, :], ln1_b[l][None, :],
                                                 wr[l], br[l], tm, n_ctx_tiles, alpha)
        cnt = counts[0, :N_EXPERTS].astype(jnp.int32)
        padded = (cnt + rows - 1) // rows * rows
        pad_end = jnp.cumsum(padded)
        pad_start = pad_end - padded
        info_t = info_t.reshape(n_tiles, 8, tm)
        eid = info_t[:, 0:2, :].astype(jnp.int32)
        rank = info_t[:, 4:6, :].astype(jnp.int32)
        dest = rank
        for e in range(N_EXPERTS):
            dest = dest + jnp.where(eid == e, pad_start[e], 0)
        pad_key = (pad_start + cnt)[:, None] + pad_j
        pad_key = jnp.where(pad_j < (padded - cnt)[:, None], pad_key, n_slots)
        keys = jnp.concatenate([dest.reshape(-1), pad_key.reshape(-1)])
        vals = jnp.concatenate([tok_ids, jnp.zeros((N_EXPERTS * (rows - 1),), jnp.int32)])
        _, slot_tok = lax.sort_key_val(keys, vals)
        slot_tok = jnp.concatenate([slot_tok, jnp.zeros((n_slots - slot_tok.shape[0],), jnp.int32)])
        block_e = jnp.sum((blk_start[:, None] >= pad_end[None, :]).astype(jnp.int32), axis=1)
        block_e = jnp.minimum(block_e, N_EXPERTS - 1)
        n_used = (pad_end[-1:] // rows).astype(jnp.int32)

        yb = _experts(h.reshape(n_tok, D // 2), slot_tok.reshape(n_blocks // IDX_GROUP, IDX_GROUP, rows), block_e, n_used,
                      w_gate, w_up, w_down, l, rows)
        xa = _combine_ln(x1, mod, info, dest, yb, ln2_g[l][None, :], ln2_b[l][None, :],
                         tm, n_ctx_tiles, alpha, skip_ctx=(l == L - 1))
    return xa
```

```python
import functools

import jax
import jax.numpy as jnp
from jax import lax
from jax.experimental import pallas as pl
from jax.experimental.pallas import tpu as pltpu

GRID_W = 64
ROPE_BASE = 10000.0
RMS_EPS = 1e-6
LN_EPS = 1e-5

MLA_HEADS = 8
MLA_Q_LORA = 384
MLA_KV_LORA = 256
MLA_NOPE = 64
MLA_ROPE = 32
MLA_V = 64
MLA_QK = MLA_NOPE + MLA_ROPE
MLA_SCALE = MLA_QK ** -0.5

HEAD_DIM = 64
GQA_HEADS = 8
GQA_KV_HEADS = 2
GQA_SCALE = HEAD_DIM ** -0.5
LOG2E = 1.4426950408889634

N_GROUPS = 4
EXPERTS_PER_GROUP = 8
N_EXPERTS = N_GROUPS * EXPERTS_PER_GROUP
TOP_K = 2

LANES = 128
MOE_ROWS = 512
IDX_GROUP = 8
VMEM_LIMIT = 56 * 1024 * 1024

_QA0, _KVA0, _KR0, _GQ0, _GK0, _GV0, _INW = 0, 384, 640, 768, 1280, 1408, 1536

_f32 = jnp.float32
_bf16 = jnp.bfloat16


def _cparams(sem):
    return pltpu.CompilerParams(dimension_semantics=sem, vmem_limit_bytes=VMEM_LIMIT)


def _sigmoid(v):
    return 1.0 / (1.0 + jnp.exp(-v))


def _layer_norm(v, g, b):
    mu = jnp.mean(v, axis=-1, keepdims=True)
    d = v - mu
    var = jnp.mean(d * d, axis=-1, keepdims=True)
    return d * lax.rsqrt(var + LN_EPS) * g + b


def _pack_bf16_pairs(v):
    w = v.shape[-1] // 2
    hi = lax.bitcast_convert_type(v[:, :w].astype(_bf16).astype(_f32), jnp.uint32)
    lo = lax.bitcast_convert_type(v[:, w:].astype(_bf16).astype(_f32), jnp.uint32)
    return hi | (lo >> 16)


def _unpack_bf16_pairs(u):
    hi = lax.bitcast_convert_type(u & jnp.uint32(0xFFFF0000), _f32)
    lo = lax.bitcast_convert_type(u << 16, _f32)
    return jnp.concatenate([hi, lo], axis=-1)


def _rot_half(y, dist):
    lane = lax.broadcasted_iota(jnp.int32, y.shape, y.ndim - 1)
    n = y.shape[-1]
    fwd = pltpu.roll(y, n - dist, y.ndim - 1)
    bwd = pltpu.roll(y, dist, y.ndim - 1)
    return jnp.where((lane & dist) == 0, fwd, bwd)


def _adaln_kernel(c_ref, w_ref, b_ref, o_ref):
    cc = c_ref[...]
    s = (cc * _sigmoid(cc)).astype(_bf16)
    o_ref[0] = jnp.dot(s, w_ref[0].astype(_bf16), preferred_element_type=_f32) + b_ref[0]


def _adaln(cc, w_ada, b_ada):
    L, D, N = w_ada.shape
    R = cc.shape[0]
    tn = 1536 if N % 1536 == 0 else N
    return pl.pallas_call(
        _adaln_kernel,
        grid=(L, N // tn),
        in_specs=[pl.BlockSpec((R, D), lambda l, j: (0, 0)),
                  pl.BlockSpec((1, D, tn), lambda l, j: (l, 0, j)),
                  pl.BlockSpec((1, 1, tn), lambda l, j: (l, 0, j))],
        out_specs=pl.BlockSpec((1, R, tn), lambda l, j: (l, 0, j)),
        out_shape=jax.ShapeDtypeStruct((L, R, N), _f32),
        compiler_params=_cparams(("arbitrary", "arbitrary")),
    )(cc, w_ada, b_ada.reshape(L, 1, N))


def _inproj_kernel(x_ref, mod_ref, win_ref, wqb_ref, wkvb_ref, gqa_ref, gkva_ref, hs_ref,
                   aqm_ref, bqm_ref, akm_ref, bkm_ref, aqg_ref, bqg_ref, akg_ref, bkg_ref,
                   qm_ref, km_ref, vm_ref, qg_ref, kg_ref, vg_ref):
    x = x_ref[0]
    h = (x * (1.0 + mod_ref[1:2, :]) + mod_ref[0:1, :]).astype(_bf16)
    z = jnp.dot(h, win_ref[...], preferred_element_type=_f32)

    q_a = z[:, _QA0:_KVA0]
    qn = q_a * lax.rsqrt(jnp.mean(q_a * q_a, axis=-1, keepdims=True) + RMS_EPS) * gqa_ref[...]
    q = jnp.dot(qn.astype(_bf16), wqb_ref[...], preferred_element_type=_f32)
    kv_a = z[:, _KVA0:_KR0]
    kvn = kv_a * lax.rsqrt(jnp.mean(kv_a * kv_a, axis=-1, keepdims=True) + RMS_EPS) * gkva_ref[...]
    kv = jnp.dot(kvn.astype(_bf16), wkvb_ref[...], preferred_element_type=_f32)
    aq, bq = aqm_ref[...], bqm_ref[...]
    for hd in range(MLA_HEADS):
        qh = q[:, hd * LANES:(hd + 1) * LANES]
        qm_ref[0, hd] = (qh * aq + _rot_half(qh, 8) * bq).astype(_bf16)
    kr = z[:, _KR0:_GQ0]
    kr = kr * akm_ref[...] + _rot_half(kr, 8) * bkm_ref[...]
    for hd in range(MLA_HEADS):
        km_ref[0, hd] = (kv[:, hd * LANES:(hd + 1) * LANES] + kr).astype(_bf16)
    v0 = MLA_HEADS * LANES
    ones = jnp.ones((x.shape[0], LANES), _bf16)
    for p in range(MLA_HEADS // 2):
        vm_ref[0, p] = jnp.concatenate([kv[:, v0 + p * LANES:v0 + (p + 1) * LANES].astype(_bf16), ones], axis=-1)

    def head_rs(t, hs):
        ssum = jnp.dot((t * t).astype(_bf16), hs, preferred_element_type=_f32)
        return lax.rsqrt(ssum * (1.0 / HEAD_DIM) + RMS_EPS)

    gq = z[:, _GQ0:_GK0]
    yq = gq * head_rs(gq, hs_ref[...])
    aqg, bqg = aqg_ref[...], bqg_ref[...]
    for j in range(GQA_HEADS // 2):
        yj = yq[:, j * LANES:(j + 1) * LANES]
        qg_ref[0, j] = (yj * aqg + _rot_half(yj, 16) * bqg).astype(_bf16)
    gk = z[:, _GK0:_GV0]
    yk = gk * head_rs(gk, hs_ref[0:LANES, 0:LANES])
    kg_ref[0] = (yk * akg_ref[...] + _rot_half(yk, 16) * bkg_ref[...]).astype(_bf16)
    vg_ref[0] = jnp.concatenate([z[:, _GV0:_INW].astype(_bf16), ones], axis=-1)


def _inproj(xa, mod, win, wqb, wkvb, gqa, gkva, hs, tabs_m, tabs_g, tm, n_ctx_tiles):
    B, T, D = xa.shape
    nt = T // tm
    full = lambda a: pl.BlockSpec(a.shape, lambda b, i: (0,) * a.ndim)
    tab = pl.BlockSpec((tm, LANES), lambda b, i: (i, 0))
    out4 = lambda n, w=LANES: pl.BlockSpec((1, n, tm, w), lambda b, i: (b, 0, i, 0))
    out3 = lambda w=LANES: pl.BlockSpec((1, tm, w), lambda b, i: (b, i, 0))
    sds = lambda *s: jax.ShapeDtypeStruct(s, _bf16)
    return pl.pallas_call(
        _inproj_kernel,
        grid=(B, nt),
        in_specs=[pl.BlockSpec((1, tm, D), lambda b, i: (b, i, 0)),
                  pl.BlockSpec((None, None, 6, D), lambda b, i: (b, jnp.where(i >= n_ctx_tiles, 1, 0), 0, 0)),
                  full(win), full(wqb), full(wkvb), full(gqa), full(gkva), full(hs),
                  tab, tab, tab, tab, tab, tab, tab, tab],
        out_specs=[out4(8), out4(8), out4(4, 2 * LANES), out4(4), out3(), out3(2 * LANES)],
        out_shape=[sds(B, 8, T, LANES), sds(B, 8, T, LANES), sds(B, 4, T, 2 * LANES),
                   sds(B, 4, T, LANES), sds(B, T, LANES), sds(B, T, 2 * LANES)],
        compiler_params=_cparams(("arbitrary", "arbitrary")),
    )(xa, mod, win, wqb, wkvb, gqa, gkva, hs, *tabs_m, *tabs_g)


def _softmax_pv(q, k, v_ext, wide):
    s = lax.dot_general(q, k, (((1,), (1,)), ((), ())), preferred_element_type=_f32)
    m = jnp.max(s, axis=-1, keepdims=True)
    if wide:
        p = jnp.exp2((s - m).astype(_bf16))
        pv = jnp.dot(p, v_ext, preferred_element_type=_f32)
        return pv[:, 0:LANES] * (1.0 / pv[:, LANES:LANES + 1])
    e = jnp.exp2(s - m)
    l = jnp.sum(e, axis=-1, keepdims=True)
    return jnp.dot(e.astype(_bf16), v_ext[:, 0:LANES], preferred_element_type=_f32) * (1.0 / l)


def _attn_kernel(qm_ref, km_ref, vm_ref, qg_ref, kg_ref, vg_ref, o_ref, *, n_ctx, n_ctx_tiles):
    tq = qm_ref.shape[2]
    n_all = km_ref.shape[2]
    lane = lax.broadcasted_iota(jnp.int32, (tq, LANES), 1)
    low = lane < (LANES // 2)

    def run(klen, wide):
        def mla_pair(p, carry):
            v = vm_ref[0, p, 0:klen, :]
            o0 = _softmax_pv(qm_ref[0, 2 * p], km_ref[0, 2 * p, 0:klen, :], v, wide)
            o1 = _softmax_pv(qm_ref[0, 2 * p + 1], km_ref[0, 2 * p + 1, 0:klen, :], v, wide)
            o_ref[0, p] = jnp.where(low, o0, o1).astype(_bf16)
            return carry

        lax.fori_loop(0, MLA_HEADS // 2, mla_pair, 0, unroll=True)

        def gqa_slab(j, carry):
            qs = qg_ref[0, j]
            k = kg_ref[0, 0:klen, :]
            v = vg_ref[0, 0:klen, :]
            o0 = _softmax_pv(jnp.where(low, qs, jnp.zeros_like(qs)), k, v, wide)
            o1 = _softmax_pv(jnp.where(low, jnp.zeros_like(qs), qs), k, v, wide)
            o_ref[0, MLA_HEADS // 2 + j] = jnp.where(low, o0, o1).astype(_bf16)
            return carry

        lax.fori_loop(0, GQA_HEADS // 2, gqa_slab, 0, unroll=True)

    i = pl.program_id(1)

    @pl.when(i < n_ctx_tiles)
    def _():
        run(n_ctx, False)

    @pl.when(i >= n_ctx_tiles)
    def _():
        run(n_all, True)


def _attention(qm, km, vm, qg, kg, vg, tq, n_ctx):
    B, _, T, _ = qm.shape
    qspec = lambda n: pl.BlockSpec((1, n, tq, LANES), lambda b, i: (b, 0, i, 0))
    kspec = lambda n, w=LANES: pl.BlockSpec((1, n, T, w), lambda b, i: (b, 0, 0, 0))
    k3 = lambda w=LANES: pl.BlockSpec((1, T, w), lambda b, i: (b, 0, 0))
    return pl.pallas_call(
        functools.partial(_attn_kernel, n_ctx=n_ctx, n_ctx_tiles=n_ctx // tq),
        grid=(B, T // tq),
        in_specs=[qspec(8), kspec(8), kspec(4, 2 * LANES), qspec(4), k3(), k3(2 * LANES)],
        out_specs=qspec(8),
        out_shape=jax.ShapeDtypeStruct((B, 8, T, LANES), _bf16),
        compiler_params=_cparams(("arbitrary", "arbitrary")),
    )(qm, km, vm, qg, kg, vg)


def _wo_router_kernel(x_ref, o_ref, mod_ref, wo_ref, g_ref, b_ref, wr_ref, br_ref,
                      x1_ref, h_ref, info_ref, infot_ref, cnt_ref, carry_ref, *, alpha):
    tm = x_ref.shape[1]
    first = (pl.program_id(0) == 0) & (pl.program_id(1) == 0)

    @pl.when(first)
    def _():
        carry_ref[...] = jnp.zeros_like(carry_ref)

    o = jnp.concatenate([o_ref[0, p] for p in range(o_ref.shape[1])], axis=-1)
    mix = jnp.dot(o, wo_ref[...], preferred_element_type=_f32)
    x1 = _layer_norm(alpha * x_ref[0] + mod_ref[2:3, :] * mix, g_ref[...], b_ref[...])
    x1_ref[0] = x1

    h = x1 * (1.0 + mod_ref[4:5, :]) + mod_ref[3:4, :]
    h_ref[0] = _pack_bf16_pairs(h)
    logits = jnp.dot(h.astype(_bf16), wr_ref[...], preferred_element_type=_f32) + br_ref[...]
    lane_i = lax.broadcasted_iota(jnp.int32, (tm, LANES), 1)
    lane = lane_i.astype(_f32)
    neg = jnp.float32(-jnp.inf)
    big = jnp.float32(LANES)

    def arg_first_max(vals):
        mx = jnp.max(vals, axis=-1, keepdims=True)
        idx = jnp.min(jnp.where(vals == mx, lane, big), axis=-1, keepdims=True)
        return mx, idx

    is_grp = (lane_i >= N_EXPERTS) & (lane_i < N_EXPERTS + N_GROUPS)
    lg = jnp.where(is_grp, logits, neg)
    gmax, gidx = arg_first_max(lg)
    p_star = 1.0 / jnp.sum(jnp.exp(lg - gmax), axis=-1, keepdims=True)
    g_star = gidx - N_EXPERTS
    in_grp = (lane_i < N_EXPERTS) & ((lane_i >> 3).astype(_f32) == g_star)
    le = jnp.where(in_grp, logits, neg)
    v1, e1 = arg_first_max(le)
    v2, e2 = arg_first_max(jnp.where(lane == e1, neg, le))
    t = jnp.exp(v2 - v1)
    w1 = p_star / (1.0 + t)
    w2 = p_star * t / (1.0 + t)

    onehot = ((lane == e1) | (lane == e2)).astype(_f32)
    r = lax.broadcasted_iota(jnp.int32, (tm, tm), 0)
    c = lax.broadcasted_iota(jnp.int32, (tm, tm), 1)
    tri = (c < r).astype(_bf16)
    prefix = jnp.dot(tri, onehot.astype(_bf16), preferred_element_type=_f32) + carry_ref[...]
    r1 = jnp.sum(jnp.where(lane == e1, prefix, 0.0), axis=-1, keepdims=True)
    r2 = jnp.sum(jnp.where(lane == e2, prefix, 0.0), axis=-1, keepdims=True)
    carry_ref[...] += jnp.sum(onehot, axis=0, keepdims=True)
    cnt_ref[...] = carry_ref[...]

    cols = [e1, e2, w1, w2, r1, r2]
    info = jnp.zeros((tm, LANES), _f32)
    for k, col in enumerate(cols):
        info = jnp.where(lane_i == k, col, info)
    info_ref[0] = info[:, 0:8]
    infot_ref[0, 0] = info.T[0:8, :]


def _wo_router(xa, o, mod, wo, g, b, wr, br, tm, n_ctx_tiles, alpha):
    B, T, D = xa.shape
    full = lambda a: pl.BlockSpec(a.shape, lambda bb, i: (0,) * a.ndim)
    row = pl.BlockSpec((1, tm, D), lambda bb, i: (bb, i, 0))
    return pl.pallas_call(
        functools.partial(_wo_router_kernel, alpha=alpha),
        grid=(B, T // tm),
        in_specs=[row,
                  pl.BlockSpec((1, 8, tm, LANES), lambda bb, i: (bb, 0, i, 0)),
                  pl.BlockSpec((None, None, 6, D), lambda bb, i: (bb, jnp.where(i >= n_ctx_tiles, 1, 0), 0, 0)),
                  full(wo), full(g), full(b), full(wr), full(br)],
        out_specs=[row, pl.BlockSpec((1, tm, D // 2), lambda bb, i: (bb, i, 0)),
                   pl.BlockSpec((1, tm, 8), lambda bb, i: (bb, i, 0)),
                   pl.BlockSpec((1, 1, 8, tm), lambda bb, i: (bb, i, 0, 0)),
                   pl.BlockSpec((1, LANES), lambda bb, i: (0, 0))],
        out_shape=[jax.ShapeDtypeStruct((B, T, D), _f32),
                   jax.ShapeDtypeStruct((B, T, D // 2), jnp.uint32),
                   jax.ShapeDtypeStruct((B, T, 8), _f32),
                   jax.ShapeDtypeStruct((B, T // tm, 8, tm), _f32),
                   jax.ShapeDtypeStruct((1, LANES), _f32)],
        scratch_shapes=[pltpu.VMEM((1, LANES), _f32)],
        compiler_params=_cparams(("arbitrary", "arbitrary")),
    )(xa, o, mod, wo, g, b, wr, br)


def _row_copy(src_ref, row, dst_ref, slot, sem):
    return pltpu.make_async_copy(src_ref.at[pl.ds(row, 1), :], dst_ref.at[pl.ds(slot, 1), :], sem)


def _issue_rows(src_ref, idx_ref, j, dst_ref, n, sem):
    for r in range(n):
        _row_copy(src_ref, idx_ref[0, j, r], dst_ref, r, sem).start(priority=r % 2)


def _wait_rows(src_ref, dst_ref, n, sem):
    for r in range(n):
        _row_copy(src_ref, 0, dst_ref, r, sem).wait()


def _weight_copies(w_refs, layer, e, bufs, slot, wsem):
    return [pltpu.make_async_copy(w.at[layer, e], b.at[slot], wsem.at[slot, k])
            for k, (w, b) in enumerate(zip(w_refs, bufs))]


def _expert_kernel(be_ref, ne_ref, nu_ref, tok_ref, h_ref, wg_ref, wu_ref, wd_ref, y_ref,
                   xbuf0, xbuf1, xbuf2, wg_f32, wu_f32, wd_f32, wg_bf, wu_bf, wd_bf, seg_ref, sem, wsem,
                   *, layer):
    i = pl.program_id(0)
    rows = xbuf0.shape[0]
    n_used = nu_ref[0]
    active = i < n_used
    draining = (i == n_used) | (i == n_used + 1)
    phase = i % 3
    bufs = (xbuf0, xbuf1, xbuf2)
    ahead = jnp.minimum(i + 2, pl.num_programs(0) - 1) % IDX_GROUP
    w_refs = (wg_ref, wu_ref, wd_ref)
    w_f32 = (wg_f32, wu_f32, wd_f32)

    @pl.when(i == 0)
    def _():
        seg_ref[0] = 0
        for c in _weight_copies(w_refs, layer, be_ref[0], w_f32, 0, wsem):
            c.start()
        _issue_rows(h_ref, tok_ref, 0, xbuf0, rows, sem.at[0])
        _issue_rows(h_ref, tok_ref, 1, xbuf1, rows, sem.at[1])

    new_expert = active & ((i == 0) | (be_ref[i] != be_ref[jnp.maximum(i - 1, 0)]))

    @pl.when(new_expert)
    def _():
        slot = seg_ref[0] % 2
        for c in _weight_copies(w_refs, layer, be_ref[i], w_f32, slot, wsem):
            c.wait()
        wg_bf[...] = wg_f32[slot].astype(_bf16)
        wu_bf[...] = wu_f32[slot].astype(_bf16)
        wd_bf[...] = wd_f32[slot].astype(_bf16)
        for c in _weight_copies(w_refs, layer, ne_ref[i], w_f32, 1 - slot, wsem):
            c.start()
        seg_ref[0] = seg_ref[0] + 1

    def step(p):
        q = (p + 2) % 3
        _wait_rows(h_ref, bufs[p], rows, sem.at[p])
        xb = _unpack_bf16_pairs(bufs[p][...]).astype(_bf16)
        _issue_rows(h_ref, tok_ref, ahead, bufs[q], rows, sem.at[q])
        g = jnp.dot(xb, wg_bf[...], preferred_element_type=_f32)
        u = jnp.dot(xb, wu_bf[...], preferred_element_type=_f32)
        a = (g * _sigmoid(g) * u).astype(_bf16)
        y_ref[...] = _pack_bf16_pairs(jnp.dot(a, wd_bf[...], preferred_element_type=_f32))

    for p in range(3):
        pl.when(active & (phase == p))(functools.partial(step, p))
        pl.when(draining & (phase == p))(
            functools.partial(_wait_rows, h_ref, bufs[p], rows, sem.at[p]))

    @pl.when(i == n_used)
    def _():
        for c in _weight_copies(w_refs, layer, be_ref[0], w_f32, seg_ref[0] % 2, wsem):
            c.wait()

    @pl.when(jnp.logical_not(active))
    def _():
        y_ref[...] = jnp.zeros_like(y_ref)


def _experts(h2d, slot_tok, block_e, next_e, n_used, w_gate, w_up, w_down, layer, rows):
    _, _, D, F = w_gate.shape
    n_blocks = slot_tok.shape[0] * IDX_GROUP
    tok = pl.BlockSpec((1, IDX_GROUP, rows), lambda i, be, ne, nu: (jnp.minimum(i + 2, n_blocks - 1) // IDX_GROUP, 0, 0),
                       memory_space=pltpu.SMEM)
    anyspec = pl.BlockSpec(memory_space=pl.ANY)
    grid_spec = pltpu.PrefetchScalarGridSpec(
        num_scalar_prefetch=3,
        grid=(n_blocks,),
        in_specs=[tok, anyspec, anyspec, anyspec, anyspec],
        out_specs=pl.BlockSpec((rows, D // 2), lambda i, be, ne, nu: (i, 0)),
        scratch_shapes=[pltpu.VMEM((rows, D // 2), jnp.uint32), pltpu.VMEM((rows, D // 2), jnp.uint32),
                        pltpu.VMEM((rows, D // 2), jnp.uint32),
                        pltpu.VMEM((2, D, F), _f32), pltpu.VMEM((2, D, F), _f32), pltpu.VMEM((2, F, D), _f32),
                        pltpu.VMEM((D, F), _bf16), pltpu.VMEM((D, F), _bf16), pltpu.VMEM((F, D), _bf16),
                        pltpu.SMEM((1,), jnp.int32),
                        pltpu.SemaphoreType.DMA((3,)), pltpu.SemaphoreType.DMA((2, 3))],
    )
    return pl.pallas_call(
        functools.partial(_expert_kernel, layer=layer),
        grid_spec=grid_spec,
        out_shape=jax.ShapeDtypeStruct((n_blocks * rows, D // 2), jnp.uint32),
        compiler_params=_cparams(("arbitrary",)),
    )(block_e, next_e, n_used, slot_tok, h2d, w_gate, w_up, w_down)


def _combine_kernel(dest_ref, x_ref, mod_ref, info_ref, yb_ref, g_ref, b_ref, out_ref,
                    ybuf0, ybuf1, ybuf2, sem, *, alpha, n_tiles):
    tm = x_ref.shape[0]
    g_id = pl.program_id(0)
    active = g_id < n_tiles
    phase = g_id % 3
    bufs = (ybuf0, ybuf1, ybuf2)
    ahead = jnp.minimum(g_id + 2, n_tiles - 1) % IDX_GROUP

    @pl.when(g_id == 0)
    def _():
        _issue_rows(yb_ref, dest_ref, 0, ybuf0, 2 * tm, sem.at[0])
        _issue_rows(yb_ref, dest_ref, min(1, n_tiles - 1), ybuf1, 2 * tm, sem.at[1])

    def step(p):
        q = (p + 2) % 3
        buf = bufs[p]
        _wait_rows(yb_ref, buf, 2 * tm, sem.at[p])
        _issue_rows(yb_ref, dest_ref, ahead, bufs[q], 2 * tm, sem.at[q])
        info = info_ref[...]
        y = (info[:, 2:3] * _unpack_bf16_pairs(buf[0:tm, :])
             + info[:, 3:4] * _unpack_bf16_pairs(buf[tm:2 * tm, :]))
        out_ref[...] = _layer_norm(alpha * x_ref[...] + mod_ref[5:6, :] * y, g_ref[...], b_ref[...])

    for p in range(3):
        pl.when(active & (phase == p))(functools.partial(step, p))
        pl.when(jnp.logical_not(active) & (phase == p))(
            functools.partial(_wait_rows, yb_ref, bufs[p], 2 * tm, sem.at[p]))


def _combine_ln(x1, mod, info, dest, yb, g, b, tm, n_ctx_tiles, alpha, skip_ctx):
    B, T, D = x1.shape
    nt = T // tm
    off = n_ctx_tiles if skip_ctx else 0
    ntl = nt - off
    n_tiles = B * ntl
    clamp = lambda s: jnp.minimum(s, n_tiles - 1)
    tile = lambda s: (s // ntl) * nt + off + s % ntl
    full = lambda a: pl.BlockSpec(a.shape, lambda s: (0,) * a.ndim)
    dest = dest.reshape(B, nt, 2 * tm)[:, off:].reshape(n_tiles, 2 * tm)
    n_grp = -(-n_tiles // IDX_GROUP)
    dest = jnp.concatenate([dest, jnp.zeros((n_grp * IDX_GROUP - n_tiles, 2 * tm), jnp.int32)])
    dest = dest.reshape(n_grp, IDX_GROUP, 2 * tm)
    dspec = pl.BlockSpec((1, IDX_GROUP, 2 * tm), lambda s: (clamp(s + 2) // IDX_GROUP, 0, 0),
                         memory_space=pltpu.SMEM)
    out = pl.pallas_call(
        functools.partial(_combine_kernel, alpha=alpha, n_tiles=n_tiles),
        grid=(n_tiles + 2,),
        in_specs=[dspec,
                  pl.BlockSpec((tm, D), lambda s: (tile(clamp(s)), 0)),
                  pl.BlockSpec((None, None, 6, D),
                               lambda s: (clamp(s) // ntl, jnp.where(clamp(s) % ntl + off >= n_ctx_tiles, 1, 0), 0, 0)),
                  pl.BlockSpec((tm, 8), lambda s: (tile(clamp(s)), 0)),
                  pl.BlockSpec(memory_space=pl.ANY),
                  full(g), full(b)],
        out_specs=pl.BlockSpec((tm, D), lambda s: (clamp(s), 0)),
        out_shape=jax.ShapeDtypeStruct((n_tiles * tm, D), _f32),
        scratch_shapes=[pltpu.VMEM((2 * tm, D // 2), jnp.uint32)] * 3 + [pltpu.SemaphoreType.DMA((3,))],
        compiler_params=_cparams(("arbitrary",)),
    )(dest, x1.reshape(B * T, D), mod, info.reshape(B * T, 8), yb, g, b)
    return out.reshape(B, ntl * tm, D)


def _prep_weights(w_in, w_q_b, w_kv_b, w_o, w_router_grp, b_router_grp, w_router_exp, b_router_exp):
    L, D, _ = w_in.shape
    zeros = lambda *s: jnp.zeros((L,) + s, _f32)
    kr = jnp.concatenate([zeros(D, 64), w_in[:, :, 640:672], zeros(D, 32)], axis=-1)
    gq = w_in[:, :, 672:1184].reshape(L, D, GQA_HEADS, HEAD_DIM)
    gq = jnp.stack([gq[:, :, :4], gq[:, :, 4:]], axis=3).reshape(L, D, 512)
    win = jnp.concatenate([w_in[:, :, 0:640], kr, gq, w_in[:, :, 1184:1440]], axis=-1).astype(_bf16)

    qb = w_q_b.reshape(L, MLA_Q_LORA, MLA_HEADS, MLA_QK)
    wqb = jnp.concatenate([qb, zeros(MLA_Q_LORA, MLA_HEADS, LANES - MLA_QK)], axis=-1)
    wqb = wqb.reshape(L, MLA_Q_LORA, MLA_HEADS * LANES).astype(_bf16)

    kvb = w_kv_b.reshape(L, MLA_KV_LORA, MLA_HEADS, MLA_NOPE + MLA_V)
    kpart = jnp.concatenate([kvb[..., :MLA_NOPE], zeros(MLA_KV_LORA, MLA_HEADS, LANES - MLA_NOPE)], axis=-1)
    vpart = kvb[..., MLA_NOPE:].reshape(L, MLA_KV_LORA, MLA_HEADS * MLA_V)
    wkvb = jnp.concatenate([kpart.reshape(L, MLA_KV_LORA, MLA_HEADS * LANES), vpart], axis=-1).astype(_bf16)

    n_mla = MLA_HEADS * MLA_V
    og = w_o[:, n_mla:, :].reshape(L, GQA_HEADS, HEAD_DIM, D)
    og = jnp.stack([og[:, :4], og[:, 4:]], axis=2).reshape(L, GQA_HEADS * HEAD_DIM, D)
    wo = jnp.concatenate([w_o[:, :n_mla, :], og], axis=1).astype(_bf16)

    pad = LANES - N_EXPERTS - N_GROUPS
    wr = jnp.concatenate([w_router_exp, w_router_grp, zeros(D, pad)], axis=-1).astype(_bf16)
    br = jnp.concatenate([b_router_exp, b_router_grp, jnp.zeros((L, pad), _f32)], axis=-1).reshape(L, 1, LANES)
    return win, wqb, wkvb, wo, wr, br


def _rope_tables(S, C, g_gqa_q, g_gqa_k):
    pos = jnp.arange(S)
    row = (pos // GRID_W).astype(_f32)
    col = (pos % GRID_W).astype(_f32)

    def angles(half):
        inv = ROPE_BASE ** (-(jnp.arange(0, half, 2, dtype=_f32) / half))
        ar, ac = row[:, None] * inv, col[:, None] * inv
        return jnp.concatenate([ar, ar, ac, ac], axis=-1)

    def cos_sin(half):
        ang = angles(half)
        sign = jnp.tile(jnp.concatenate([-jnp.ones(half // 2), jnp.ones(half // 2)]), 2).astype(_f32)
        cos = jnp.concatenate([jnp.ones((C, 2 * half), _f32), jnp.cos(ang)], axis=0)
        sin = jnp.concatenate([jnp.zeros((C, 2 * half), _f32), jnp.sin(ang) * sign], axis=0)
        return cos, sin

    T = S + C
    cm, sm = cos_sin(MLA_ROPE // 2)
    a_m = jnp.concatenate([jnp.ones((T, 64), _f32), cm, jnp.ones((T, 32), _f32)], axis=-1)
    b_m = jnp.concatenate([jnp.zeros((T, 64), _f32), sm, jnp.zeros((T, 32), _f32)], axis=-1)
    tabs_m = (a_m * (MLA_SCALE * LOG2E), b_m * (MLA_SCALE * LOG2E), a_m, b_m)

    cg, sg = cos_sin(HEAD_DIM // 2)
    swap = jnp.arange(HEAD_DIM) ^ (HEAD_DIM // 4)

    def gqa(g, scale):
        a = g[:, None, :] * cg[None] * scale
        b = g[:, swap][:, None, :] * sg[None] * scale
        return jnp.tile(a, (1, 1, 2)), jnp.tile(b, (1, 1, 2))

    aq, bq = gqa(g_gqa_q, GQA_SCALE * LOG2E)
    ak, bk = gqa(g_gqa_k, 1.0)
    return tabs_m, (aq, bq, ak, bk)


def kernel(x, c, ctx, c_ctx, w_ada, b_ada, w_in, g_q_a, w_q_b, g_kv_a, w_kv_b, g_gqa_q, g_gqa_k,
           w_o, ln1_g, ln1_b, w_router_grp, b_router_grp, w_router_exp, b_router_exp,
           w_gate, w_up, w_down, ln2_g, ln2_b):
    B, S, D = x.shape
    C = ctx.shape[1]
    L = w_in.shape[0]
    T = C + S
    alpha = (2.0 * L) ** 0.25
    tm = 256 if C % 256 == 0 else 128
    assert C % tm == 0 and S % tm == 0
    n_ctx_tiles = C // tm
    rows = MOE_ROWS

    R = -(-(B + 1) // 8) * 8
    cc = jnp.concatenate([c, c_ctx[None, :], jnp.zeros((R - B - 1, D), _f32)], axis=0)
    mods = _adaln(cc, w_ada, b_ada)
    lat = mods[:, :B].reshape(L, B, 1, 6, D)
    cx = jnp.broadcast_to(mods[:, B].reshape(L, 1, 1, 6, D), (L, B, 1, 6, D))
    mods = jnp.concatenate([cx, lat], axis=2)

    win, wqb, wkvb, wo, wr, br = _prep_weights(w_in, w_q_b, w_kv_b, w_o, w_router_grp, b_router_grp,
                                               w_router_exp, b_router_exp)
    tabs_m, tabs_g = _rope_tables(S, C, g_gqa_q, g_gqa_k)
    blk = jnp.arange(GQA_HEADS * HEAD_DIM) // HEAD_DIM
    hs = (blk[:, None] == blk[None, :]).astype(_bf16)

    n_tok = B * T
    n_tiles = n_tok // tm
    n_assign = n_tok * TOP_K
    n_blocks = -(-n_assign // rows) + N_EXPERTS + 2
    n_blocks = -(-n_blocks // IDX_GROUP) * IDX_GROUP
    n_slots = n_blocks * rows
    blk_start = jnp.arange(n_blocks, dtype=jnp.int32) * rows
    tok_ids = jnp.arange(n_tok, dtype=jnp.int32).reshape(n_tiles, 1, tm)
    tok_ids = jnp.broadcast_to(tok_ids, (n_tiles, TOP_K, tm)).reshape(-1)
    pad_j = jnp.arange(rows - 1, dtype=jnp.int32)[None, :]
    e_ids = jnp.arange(N_EXPERTS, dtype=jnp.int32)

    xa = jnp.concatenate([ctx, x], axis=1)
    for l in range(L):
        mod = mods[l]
        qm, km, vm, qg, kg, vg = _inproj(
            xa, mod, win[l], wqb[l], wkvb[l], g_q_a[l][None, :], g_kv_a[l][None, :], hs,
            tabs_m, tuple(t[l] for t in tabs_g), tm, n_ctx_tiles)
        o = _attention(qm, km, vm, qg, kg, vg, tm, C)
        x1, h, info, info_t, counts = _wo_router(xa, o, mod, wo[l], ln1_g[l][None, :], ln1_b[l][None, :],
                                                 wr[l], br[l], tm, n_ctx_tiles, alpha)
        cnt = counts[0, :N_EXPERTS].astype(jnp.int32)
        padded = (cnt + rows - 1) // rows * rows
        pad_end = jnp.cumsum(padded)
        pad_start = pad_end - padded
        info_t = info_t.reshape(n_tiles, 8, tm)
        eid = info_t[:, 0:2, :].astype(jnp.int32)
        rank = info_t[:, 4:6, :].astype(jnp.int32)
        dest = rank
        for e in range(N_EXPERTS):
            dest = dest + jnp.where(eid == e, pad_start[e], 0)
        pad_key = (pad_start + cnt)[:, None] + pad_j
        pad_key = jnp.where(pad_j < (padded - cnt)[:, None], pad_key, n_slots)
        keys = jnp.concatenate([dest.reshape(-1), pad_key.reshape(-1)])
        vals = jnp.concatenate([tok_ids, jnp.zeros((N_EXPERTS * (rows - 1),), jnp.int32)])
        _, slot_tok = lax.sort_key_val(keys, vals)
        slot_tok = jnp.concatenate([slot_tok, jnp.zeros((n_slots - slot_tok.shape[0],), jnp.int32)])
        block_e = jnp.sum((blk_start[:, None] >= pad_end[None, :]).astype(jnp.int32), axis=1)
        block_e = jnp.minimum(block_e, N_EXPERTS - 1)
        n_used = (pad_end[-1:] // rows).astype(jnp.int32)
        later = (e_ids[None, :] > e_ids[:, None]) & (padded[None, :] > 0)
        next_of = jnp.min(jnp.where(later, e_ids[None, :], N_EXPERTS), axis=1)
        next_of = jnp.where(next_of < N_EXPERTS, next_of, e_ids)
        next_e = jnp.sum(jnp.where(block_e[:, None] == e_ids[None, :], next_of[None, :], 0), axis=1)

        yb = _experts(h.reshape(n_tok, D // 2), slot_tok.reshape(n_blocks // IDX_GROUP, IDX_GROUP, rows), block_e, next_e, n_used,
                      w_gate, w_up, w_down, l, rows)
        xa = _combine_ln(x1, mod, info, dest, yb, ln2_g[l][None, :], ln2_b[l][None, :],
                         tm, n_ctx_tiles, alpha, skip_ctx=(l == L - 1))
    return xa
```
